```python
import math
import jax, jax.numpy as jnp
from jax import lax
import numpy as np

D_MODEL = 1024
BATCH = 32
SEQ = 2048
DEPTH = 4
DEC_BATCH = 8
DEC_SEQ = 4096
PAST_LEN = 128

N_MIXERS = 3
N_HYENA = (DEPTH + 2) // 3
N_RWKV = (DEPTH + 1) // 3
N_DIFF = DEPTH // 3
N_DENSE = (DEPTH + 1) // 2
N_MOE = DEPTH // 2
DEEPNORM_ALPHA = (2 * DEPTH) ** 0.25
DEEPNORM_BETA = (8 * DEPTH) ** -0.25
LN_EPS = 1e-5

HYENA_ORDER = 2
SHORT_CONV = 3
FILTER_BANDS = 16
FILTER_EMB = 1 + 2 * FILTER_BANDS
FILTER_HIDDEN = 64
HYENA_MIN_DECAY = math.log(1e-2) / 1.5
HYENA_MAX_DECAY = math.log(1e-2) / 0.3
FILTER_NORM_EPS = 1e-6

RWKV_HEAD = 64
RWKV_HEADS = D_MODEL // RWKV_HEAD
DECAY_LORA = 64
AAA_LORA = 64
GATE_LORA = 160
RWKV_GN_EPS = 64e-5

DA_HEAD = 64
DA_HEADS = D_MODEL // (2 * DA_HEAD)
ROPE_THETA = 10000.0
Q_BLOCK = 128
SUBLN_EPS = 1e-5

D_FF = 2816
N_EXPERTS = 8
TOP_K = 2
D_FF_EXPERT = 3584

kernel_name = 'hybrid_hyena_rwkv7_diffattn_encoder'

F32 = jnp.float32


def layer_norm(x, g, b):
    xf = x.astype(F32)
    mu = jnp.mean(xf, -1, keepdims=True)
    var = jnp.mean(jnp.square(xf - mu), -1, keepdims=True)
    return ((xf - mu) * lax.rsqrt(var + LN_EPS) * g.astype(F32) + b.astype(F32)).astype(x.dtype)


def centred_depthwise_conv(u, w, b):
    K, L = w.shape[0], u.shape[1]
    up = jnp.pad(u, ((0, 0), (K // 2, K // 2), (0, 0)))
    return sum(up[:, j:j + L] * w[j] for j in range(K)) + b


def hyena_filter_spectrum(L, w1, b1, w2, b2, w3, b3, w4, freq):
    t = jnp.linspace(0.0, 1.0, L, dtype=F32)[:, None]
    omega = 2.0 * math.pi * jnp.arange(L, dtype=F32)[:, None] / L
    bands = jnp.linspace(1e-4, FILTER_BANDS - 1, FILTER_BANDS, dtype=F32)[None, :]
    z = jnp.concatenate([t, jnp.cos(bands * omega), -jnp.sin(bands * omega)], -1)
    fr = freq.astype(F32)
    hf = jnp.sin(fr * (z @ w1.astype(F32) + b1.astype(F32)))
    hf = jnp.sin(fr * (hf @ w2.astype(F32) + b2.astype(F32)))
    hf = jnp.sin(fr * (hf @ w3.astype(F32) + b3.astype(F32)))
    hf = (hf @ w4.astype(F32)).reshape(L, 2, HYENA_ORDER, D_MODEL)
    deltas = jnp.abs(jnp.linspace(HYENA_MIN_DECAY, HYENA_MAX_DECAY, D_MODEL, dtype=F32))
    hf = hf * jnp.exp(-t * deltas)[:, None, None, :]
    k2 = jnp.concatenate([hf[:, 0], jnp.zeros_like(hf[:1, 0]), hf[:0:-1, 1]], axis=0)
    k2 = k2 / (jnp.sum(jnp.abs(k2), axis=0, keepdims=True) + FILTER_NORM_EPS)
    return jnp.fft.rfft(k2, axis=0)


def hyena_mixer(h, w_in, conv_w, conv_b, f_w1, f_b1, f_w2, f_b2, f_w3, f_b3, f_w4, f_freq, skip, w_out):
    B, L, D = h.shape
    u = centred_depthwise_conv(h @ w_in, conv_w, conv_b).astype(F32)
    v, x1, x2 = jnp.split(u, 3, axis=-1)
    spec = hyena_filter_spectrum(L, f_w1, f_b1, f_w2, f_b2, f_w3, f_b3, f_w4, f_freq)
    z = v
    for o, gate in enumerate((x1, x2)):
        conv = jnp.fft.irfft(jnp.fft.rfft(z, n=2 * L, axis=1) * spec[:, o], n=2 * L, axis=1)[:, :L]
        z = gate * (conv + z * skip[o].astype(F32))
    return z.astype(h.dtype) @ w_out


def centred_token_shift(x):
    xp = jnp.pad(x, ((0, 0), (1, 1), (0, 0)))
    return 0.5 * (xp[:, :-2] + xp[:, 2:]) - x


def wkv7_scan(r, w, k, v, a, b, reverse):
    B, L, H, N = r.shape

    def step(S, inp):
        r_t, w_t, k_t, v_t, a_t, b_t = inp
        sa = jnp.einsum('bhvk,bhk->bhv', S, a_t)
        S = S * w_t[:, :, None, :] + sa[..., None] * b_t[:, :, None, :] + v_t[..., None] * k_t[:, :, None, :]
        return S, jnp.einsum('bhvk,bhk->bhv', S, r_t)

    xs = tuple(jnp.moveaxis(t, 1, 0) for t in (r, w, k, v, a, b))
    _, ys = lax.scan(step, jnp.zeros((B, H, N, N), F32), xs, reverse=reverse)
    return jnp.moveaxis(ys, 0, 1)


def rwkv7_mixer(h, mu, w_rkv, w0, w1, w2, a0, a1, a2, g1, g2, k_k, k_a, r_k, gn_g, gn_b, w_out):
    B, L, D = h.shape
    H, N = RWKV_HEADS, RWKV_HEAD
    heads = lambda t: t.reshape(B, L, H, N).astype(F32)
    xx = centred_token_shift(h)
    xm = h[:, :, None, :] + xx[:, :, None, :] * mu
    rkv = jnp.einsum('bljd,jde->blje', xm[:, :, :3], w_rkv)
    r, k, v = rkv[:, :, 0], rkv[:, :, 1], rkv[:, :, 2]
    xw, xa, xg = xm[:, :, 3], xm[:, :, 4], xm[:, :, 5]
    g = jax.nn.sigmoid(xg @ g1) @ g2
    kk = heads(k * k_k)
    kk = kk / jnp.maximum(jnp.sqrt(jnp.sum(kk * kk, -1, keepdims=True)), 1e-12)
    rh, kh, vh = heads(r), heads(k), heads(v)
    k_a_h = k_a.astype(F32).reshape(H, N)
    ys, bonuses = [], []
    for d, reverse in ((0, False), (1, True)):
        w_log = -jax.nn.softplus(-(w0[d] + jnp.tanh(xw @ w1[d]) @ w2[d])) - 0.5
        decay = jnp.exp(-jnp.exp(heads(w_log)))
        a = heads(jax.nn.sigmoid(a0[d] + (xa @ a1[d]) @ a2[d]))
        kd = kh * (1.0 + (a - 1.0) * k_a_h)
        ys.append(wkv7_scan(rh, decay, kd, vh, -kk, kk * a, reverse))
        bonuses.append(jnp.sum(rh * kd * r_k[d].astype(F32).reshape(H, N), -1, keepdims=True) * vh)
    y = ys[0] + ys[1]
    mean = jnp.mean(y, -1, keepdims=True)
    var = jnp.mean(jnp.square(y - mean), -1, keepdims=True)
    yn = ((y - mean) * lax.rsqrt(var + RWKV_GN_EPS)).reshape(B, L, D) * gn_g.astype(F32) + gn_b.astype(F32)
    out = (yn + (bonuses[0] + bonuses[1]).reshape(B, L, D)) * g.astype(F32)
    return out.astype(h.dtype) @ w_out


def rope(x):
    B, L, nh, d = x.shape
    inv = ROPE_THETA ** (-jnp.arange(0, d, 2, dtype=F32) / d)
    ang = jnp.arange(L, dtype=F32)[:, None] * inv[None, :]
    cos, sin = jnp.cos(ang)[None, :, None, :], jnp.sin(ang)[None, :, None, :]
    xf = x.astype(F32)
    x1, x2 = xf[..., :d // 2], xf[..., d // 2:]
    return jnp.concatenate([x1 * cos - x2 * sin, x2 * cos + x1 * sin], -1).astype(x.dtype)


def diff_attention(h, layer_idx, w_qkv, lam, subln_g, w_out):
    B, L, D = h.shape
    H, d = DA_HEADS, DA_HEAD
    q, k, v = jnp.split(h @ w_qkv, 3, axis=-1)
    q = rope(q.reshape(B, L, 2 * H, d)) * (d ** -0.5)
    k = rope(k.reshape(B, L, 2 * H, d))
    v = v.reshape(B, L, H, 2 * d)
    lam_init = 0.8 - 0.6 * math.exp(-0.3 * layer_idx)
    lf = lam.astype(F32)
    lam_full = jnp.exp(jnp.sum(lf[0] * lf[1])) - jnp.exp(jnp.sum(lf[2] * lf[3])) + lam_init
    nb = L // Q_BLOCK
    qb = jnp.moveaxis(q.reshape(B, nb, Q_BLOCK, 2 * H, d), 1, 0)

    def block(q_blk):
        s = jnp.einsum('bqhd,bkhd->bhqk', q_blk, k).astype(F32)
        p = jax.nn.softmax(s, axis=-1).reshape(B, H, 2, Q_BLOCK, L)
        a = p[:, :, 0] - lam_full * p[:, :, 1]
        return jnp.einsum('bhqk,bkhe->bqhe', a.astype(v.dtype), v)

    o = jnp.moveaxis(lax.map(block, qb), 0, 1).reshape(B, L, H, 2 * d).astype(F32)
    o = o * lax.rsqrt(jnp.mean(o * o, -1, keepdims=True) + SUBLN_EPS) * subln_g.astype(F32) * (1.0 - lam_init)
    return o.reshape(B, L, D).astype(h.dtype) @ w_out


def swiglu(h, w_gu, w_down):
    gt, up = jnp.split(h @ w_gu, 2, axis=-1)
    return (jax.nn.silu(gt) * up) @ w_down


def moe_swiglu(h, router, w_gu, w_down):
    B, L, D = h.shape
    t = h.reshape(B * L, D)
    logits = (t @ router).astype(F32)
    top_val, top_idx = lax.top_k(logits, TOP_K)
    gates = jax.nn.softmax(top_val, axis=-1)
    combine = jnp.einsum('tk,tke->te', gates, jax.nn.one_hot(top_idx, N_EXPERTS, dtype=F32)).astype(t.dtype)
    out = jnp.zeros_like(t)
    for e in range(N_EXPERTS):
        out = out + combine[:, e:e + 1] * swiglu(t, w_gu[e], w_down[e])
    return out.reshape(B, L, D)


def trunk(x, c, P):
    cs = jax.nn.silu(c)
    for i in range(DEPTH):
        mod = (cs @ P['ada_w'][i] + P['ada_b'][i])[:, None, :]
        sh1, sc1, gt1, sh2, sc2, gt2 = jnp.split(mod, 6, axis=-1)
        hm = x * (1.0 + sc1) + sh1
        kind, j = i % N_MIXERS, i // N_MIXERS
        if kind == 0:
            y = hyena_mixer(hm, P['hy_w_in'][j], P['hy_conv_w'][j], P['hy_conv_b'][j],
                            P['hy_f_w1'][j], P['hy_f_b1'][j], P['hy_f_w2'][j], P['hy_f_b2'][j],
                            P['hy_f_w3'][j], P['hy_f_b3'][j], P['hy_f_w4'][j], P['hy_f_freq'][j],
                            P['hy_bias'][j], P['hy_w_out'][j])
        elif kind == 1:
            y = rwkv7_mixer(hm, P['rw_mu'][j], P['rw_w_rkv'][j], P['rw_w0'][j], P['rw_w1'][j], P['rw_w2'][j],
                            P['rw_a0'][j], P['rw_a1'][j], P['rw_a2'][j], P['rw_g1'][j], P['rw_g2'][j],
                            P['rw_k_k'][j], P['rw_k_a'][j], P['rw_r_k'][j], P['rw_gn_g'][j], P['rw_gn_b'][j],
                            P['rw_w_out'][j])
        else:
            y = diff_attention(hm, i, P['da_w_qkv'][j], P['da_lam'][j], P['da_subln_g'][j], P['da_w_out'][j])
        x = layer_norm(DEEPNORM_ALPHA * x + (1.0 + gt1) * y, P['ln_g'][i, 0], P['ln_b'][i, 0])
        hf = x * (1.0 + sc2) + sh2
        if i % 2 == 0:
            y = swiglu(hf, P['ff_w_gu'][i // 2], P['ff_w_down'][i // 2])
        else:
            y = moe_swiglu(hf, P['moe_router'][i // 2], P['moe_w_gu'][i // 2], P['moe_w_down'][i // 2])
        x = layer_norm(DEEPNORM_ALPHA * x + (1.0 + gt2) * y, P['ln_g'][i, 1], P['ln_b'][i, 1])
    return x


def setup_inputs(seed: int = 0) -> dict:
    key = jax.random.key(seed)
    keys = jax.random.split(key, 64)
    counter = [0]

    def nxt():
        kk = keys[counter[0]]
        counter[0] += 1
        return kk

    def nrm(shape, scale=1.0):
        return jax.random.normal(nxt(), shape, F32) * scale

    def uni(shape, lo, hi):
        return jax.random.uniform(nxt(), shape, F32, lo, hi)

    D = D_MODEL
    beta = DEEPNORM_BETA
    return {
        'x_prompt': nrm((BATCH, SEQ, D)),
        'x_sample': nrm((DEC_BATCH, DEC_SEQ, D)),
        'c_prompt': nrm((BATCH, D)),
        'c_sample': nrm((DEC_BATCH, D)),
        'ada_w': nrm((DEPTH, D, 6 * D), 0.2 * D ** -0.5),
        'ada_b': nrm((DEPTH, 6 * D), 0.02),
        'ln_g': 1.0 + nrm((DEPTH, 2, D), 0.02),
        'ln_b': nrm((DEPTH, 2, D), 0.02),
        'hy_w_in': nrm((N_HYENA, D, 3 * D), D ** -0.5),
        'hy_conv_w': nrm((N_HYENA, SHORT_CONV, 3 * D), SHORT_CONV ** -0.5),
        'hy_conv_b': nrm((N_HYENA, 3 * D), 0.02),
        'hy_f_w1': nrm((N_HYENA, FILTER_EMB, FILTER_HIDDEN), FILTER_EMB ** -0.5),
        'hy_f_b1': nrm((N_HYENA, FILTER_HIDDEN), 0.1),
        'hy_f_w2': nrm((N_HYENA, FILTER_HIDDEN, FILTER_HIDDEN), FILTER_HIDDEN ** -0.5),
        'hy_f_b2': nrm((N_HYENA, FILTER_HIDDEN), 0.1),
        'hy_f_w3': nrm((N_HYENA, FILTER_HIDDEN, FILTER_HIDDEN), FILTER_HIDDEN ** -0.5),
        'hy_f_b3': nrm((N_HYENA, FILTER_HIDDEN), 0.1),
        'hy_f_w4': nrm((N_HYENA, FILTER_HIDDEN, 2 * HYENA_ORDER * D), FILTER_HIDDEN ** -0.5),
        'hy_f_freq': 1.0 + nrm((N_HYENA, FILTER_HIDDEN), 0.1),
        'hy_bias': nrm((N_HYENA, HYENA_ORDER, D), 1.0),
        'hy_w_out': nrm((N_HYENA, D, D), beta * D ** -0.5),
        'rw_mu': uni((N_RWKV, 6, D), 0.0, 1.0),
        'rw_w_rkv': nrm((N_RWKV, 3, D, D), D ** -0.5),
        'rw_w0': uni((N_RWKV, 2, D), -6.0, -1.0),
        'rw_w1': nrm((N_RWKV, 2, D, DECAY_LORA), D ** -0.5),
        'rw_w2': nrm((N_RWKV, 2, DECAY_LORA, D), 0.1 * DECAY_LORA ** -0.5),
        'rw_a0': nrm((N_RWKV, 2, D), 0.1),
        'rw_a1': nrm((N_RWKV, 2, D, AAA_LORA), D ** -0.5),
        'rw_a2': nrm((N_RWKV, 2, AAA_LORA, D), 0.1 * AAA_LORA ** -0.5),
        'rw_g1': nrm((N_RWKV, D, GATE_LORA), D ** -0.5),
        'rw_g2': nrm((N_RWKV, GATE_LORA, D), GATE_LORA ** -0.5),
        'rw_k_k': 0.85 + nrm((N_RWKV, D), 0.02),
        'rw_k_a': 1.0 + nrm((N_RWKV, D), 0.02),
        'rw_r_k': nrm((N_RWKV, 2, D), 0.1),
        'rw_gn_g': 1.0 + nrm((N_RWKV, D), 0.02),
        'rw_gn_b': nrm((N_RWKV, D), 0.02),
        'rw_w_out': nrm((N_RWKV, D, D), beta * D ** -0.5),
        'da_w_qkv': nrm((N_DIFF, D, 3 * D), D ** -0.5),
        'da_lam': nrm((N_DIFF, 4, DA_HEAD), 0.1),
        'da_subln_g': 1.0 + nrm((N_DIFF, 2 * DA_HEAD), 0.02),
        'da_w_out': nrm((N_DIFF, D, D), beta * D ** -0.5),
        'ff_w_gu': nrm((N_DENSE, D, 2 * D_FF), D ** -0.5),
        'ff_w_down': nrm((N_DENSE, D_FF, D), beta * D_FF ** -0.5),
        'moe_router': nrm((N_MOE, D, N_EXPERTS), D ** -0.5),
        'moe_w_gu': nrm((N_MOE, N_EXPERTS, D, 2 * D_FF_EXPERT), D ** -0.5),
        'moe_w_down': nrm((N_MOE, N_EXPERTS, D_FF_EXPERT, D), beta * D_FF_EXPERT ** -0.5),
    }


def reference(x_prompt, x_sample, c_prompt, c_sample, ada_w, ada_b, ln_g, ln_b,
              hy_w_in, hy_conv_w, hy_conv_b, hy_f_w1, hy_f_b1, hy_f_w2, hy_f_b2, hy_f_w3, hy_f_b3,
              hy_f_w4, hy_f_freq, hy_bias, hy_w_out,
              rw_mu, rw_w_rkv, rw_w0, rw_w1, rw_w2, rw_a0, rw_a1, rw_a2, rw_g1, rw_g2,
              rw_k_k, rw_k_a, rw_r_k, rw_gn_g, rw_gn_b, rw_w_out,
              da_w_qkv, da_lam, da_subln_g, da_w_out,
              ff_w_gu, ff_w_down, moe_router, moe_w_gu, moe_w_down):
    P = dict(ada_w=ada_w, ada_b=ada_b, ln_g=ln_g, ln_b=ln_b,
             hy_w_in=hy_w_in, hy_conv_w=hy_conv_w, hy_conv_b=hy_conv_b,
             hy_f_w1=hy_f_w1, hy_f_b1=hy_f_b1, hy_f_w2=hy_f_w2, hy_f_b2=hy_f_b2,
             hy_f_w3=hy_f_w3, hy_f_b3=hy_f_b3, hy_f_w4=hy_f_w4, hy_f_freq=hy_f_freq,
             hy_bias=hy_bias, hy_w_out=hy_w_out,
             rw_mu=rw_mu, rw_w_rkv=rw_w_rkv, rw_w0=rw_w0, rw_w1=rw_w1, rw_w2=rw_w2,
             rw_a0=rw_a0, rw_a1=rw_a1, rw_a2=rw_a2, rw_g1=rw_g1, rw_g2=rw_g2,
             rw_k_k=rw_k_k, rw_k_a=rw_k_a, rw_r_k=rw_r_k, rw_gn_g=rw_gn_g, rw_gn_b=rw_gn_b,
             rw_w_out=rw_w_out,
             da_w_qkv=da_w_qkv, da_lam=da_lam, da_subln_g=da_subln_g, da_w_out=da_w_out,
             ff_w_gu=ff_w_gu, ff_w_down=ff_w_down,
             moe_router=moe_router, moe_w_gu=moe_w_gu, moe_w_down=moe_w_down)
    y_prompt = trunk(x_prompt, c_prompt, P)
    y_sample = trunk(x_sample, c_sample, P)
    return (y_prompt, y_sample)
```

```python
import functools
import math

import jax
import jax.numpy as jnp
from jax import lax
from jax.experimental import pallas as pl
from jax.experimental.pallas import tpu as pltpu

F32 = jnp.float32
BF16 = jnp.bfloat16

D_MODEL = 1024
DEPTH = 4
N_MIXERS = 3
DEEPNORM_ALPHA = (2 * DEPTH) ** 0.25
LN_EPS = 1e-5

HYENA_ORDER = 2
FILTER_BANDS = 16
FILTER_HIDDEN = 64
HYENA_MIN_DECAY = math.log(1e-2) / 1.5
HYENA_MAX_DECAY = math.log(1e-2) / 0.3
FILTER_NORM_EPS = 1e-6

RWKV_HEAD = 64
RWKV_HEADS = D_MODEL // RWKV_HEAD
RWKV_LORA = 64
GATE_LORA = 160
RWKV_GN_EPS = 64e-5

DA_HEAD = 64
DA_HEADS = D_MODEL // (2 * DA_HEAD)
ROPE_THETA = 10000.0
SUBLN_EPS = 1e-5

D_FF = 2816
N_EXPERTS = 8
D_FF_EXPERT = 3584

LANES = 128
VMEM_LIMIT_BYTES = 56 * 1024 * 1024


def _params(*semantics):
    return pltpu.CompilerParams(dimension_semantics=semantics, vmem_limit_bytes=VMEM_LIMIT_BYTES)


def _split_bf16(x):
    hi = x.astype(BF16)
    lo = (x - hi.astype(F32)).astype(BF16)
    return hi, lo


def _dot(a, b):
    return jnp.dot(a, b, preferred_element_type=F32)


def _dot_split(a, b):
    ah, al = _split_bf16(a)
    bh, bl = _split_bf16(b)
    return _dot(ah, bh) + _dot(al, bh) + _dot(ah, bl)


def _sigmoid(x):
    return 1.0 / (1.0 + jnp.exp(-x))


def _layer_norm(x, g, b):
    mu = jnp.mean(x, -1, keepdims=True)
    xc = x - mu
    var = jnp.mean(xc * xc, -1, keepdims=True)
    return xc * lax.rsqrt(var + LN_EPS) * g + b


def _mod_spec(which, tm, seq_len):
    return pl.BlockSpec((1, 1, D_MODEL), lambda i, *_: ((i * tm // seq_len) * 6 + which, 0, 0))


def _ada_kernel(c_ref, w_ref, b_ref, o_ref):
    c = c_ref[...]
    cs = (c * _sigmoid(c)).astype(BF16)
    o_ref[...] = _dot(cs, w_ref[...].astype(BF16)) + b_ref[...]


def ada_modulation(c, ada_w, ada_b):
    nb = c.shape[0]
    tn = 1024
    return pl.pallas_call(
        _ada_kernel,
        grid=(DEPTH, 6 * D_MODEL // tn),
        in_specs=[pl.BlockSpec((nb, D_MODEL), lambda l, j: (0, 0)),
                  pl.BlockSpec((None, D_MODEL, tn), lambda l, j: (l, 0, j)),
                  pl.BlockSpec((None, 1, tn), lambda l, j: (l, 0, j))],
        out_specs=pl.BlockSpec((None, nb, tn), lambda l, j: (l, 0, j)),
        out_shape=jax.ShapeDtypeStruct((DEPTH, nb, 6 * D_MODEL), F32),
        compiler_params=_params("parallel", "parallel"),
        name="ada_modulation",
    )(c, ada_w, ada_b.reshape(DEPTH, 1, 6 * D_MODEL))


def _proj_kernel(x_ref, sc_ref, sh_ref, w_ref, o_ref, xb_ref):
    @pl.when(pl.program_id(1) == 0)
    def _():
        xb_ref[...] = (x_ref[...] * (1.0 + sc_ref[0]) + sh_ref[0]).astype(BF16)

    o_ref[...] = _dot(xb_ref[...], w_ref[...]).astype(o_ref.dtype)


def _qkv_rope_kernel(x_ref, sc_ref, sh_ref, w_ref, cos_ref, sin_ref, o_ref, xb_ref, *, n_rope_tiles):
    j = pl.program_id(1)

    @pl.when(j == 0)
    def _():
        xb_ref[...] = (x_ref[...] * (1.0 + sc_ref[0]) + sh_ref[0]).astype(BF16)

    y = _dot(xb_ref[...], w_ref[...])
    tn = y.shape[1]

    @pl.when(j < n_rope_tiles)
    def _():
        lane = lax.broadcasted_iota(jnp.int32, y.shape, 1)
        first_half = (lane % DA_HEAD) < (DA_HEAD // 2)
        partner = jnp.where(first_half,
                            pltpu.roll(y, tn - DA_HEAD // 2, axis=1),
                            pltpu.roll(y, DA_HEAD // 2, axis=1))
        o_ref[...] = (y * cos_ref[...] + partner * sin_ref[...]).astype(o_ref.dtype)

    @pl.when(j >= n_rope_tiles)
    def _():
        o_ref[...] = y.astype(o_ref.dtype)


def modulated_projection(x, mod, w, seq_len, *, tm=512, tn=512):
    t_rows, n_out = x.shape[0], w.shape[1]
    return pl.pallas_call(
        _proj_kernel,
        grid=(t_rows // tm, n_out // tn),
        in_specs=[pl.BlockSpec((tm, D_MODEL), lambda i, j: (i, 0)),
                  _mod_spec(1, tm, seq_len), _mod_spec(0, tm, seq_len),
                  pl.BlockSpec((D_MODEL, tn), lambda i, j: (0, j))],
        out_specs=pl.BlockSpec((tm, tn), lambda i, j: (i, j)),
        out_shape=jax.ShapeDtypeStruct((t_rows, n_out), F32),
        scratch_shapes=[pltpu.VMEM((tm, D_MODEL), BF16)],
        compiler_params=_params("parallel", "arbitrary"),
        name="modulated_projection",
    )(x, mod, mod, w)


def qkv_rope_projection(x, mod, w, seq_len, *, tm=512, tn=512):
    t_rows = x.shape[0]
    tiles_per_seq = seq_len // tm
    nq_tiles = D_MODEL // tn
    half = DA_HEAD // 2
    inv = ROPE_THETA ** (-jnp.arange(0, DA_HEAD, 2, dtype=F32) / DA_HEAD)
    ang = jnp.arange(seq_len, dtype=F32)[:, None] * inv[None, :]
    cos_t = jnp.tile(jnp.concatenate([jnp.cos(ang), jnp.cos(ang)], -1), (1, tn // DA_HEAD))
    sin_t = jnp.tile(jnp.concatenate([-jnp.sin(ang), jnp.sin(ang)], -1), (1, tn // DA_HEAD))
    qscale = jnp.float32(DA_HEAD ** -0.5)
    cos_all = jnp.concatenate([cos_t * qscale, cos_t], 0)
    sin_all = jnp.concatenate([sin_t * qscale, sin_t], 0)
    tab = lambda i, j: (jnp.where(j < nq_tiles, 0, tiles_per_seq) + i % tiles_per_seq, 0)
    return pl.pallas_call(
        functools.partial(_qkv_rope_kernel, n_rope_tiles=2 * nq_tiles),
        grid=(t_rows // tm, 3 * D_MODEL // tn),
        in_specs=[pl.BlockSpec((tm, D_MODEL), lambda i, j: (i, 0)),
                  _mod_spec(1, tm, seq_len), _mod_spec(0, tm, seq_len),
                  pl.BlockSpec((D_MODEL, tn), lambda i, j: (0, j)),
                  pl.BlockSpec((tm, tn), tab), pl.BlockSpec((tm, tn), tab)],
        out_specs=pl.BlockSpec((tm, tn), lambda i, j: (i, j)),
        out_shape=jax.ShapeDtypeStruct((t_rows, 3 * D_MODEL), BF16),
        scratch_shapes=[pltpu.VMEM((tm, D_MODEL), BF16)],
        compiler_params=_params("parallel", "arbitrary"),
        name="qkv_rope_projection",
    )(x, mod, mod, w, cos_all, sin_all)


def _out_ln_kernel(*refs, has_zgate):
    if has_zgate:
        z_ref, zg_ref, x_ref, gt_ref, w_ref, g_ref, b_ref, o_ref = refs
        z = (z_ref[...].astype(F32) * zg_ref[...]).astype(BF16)
    else:
        z_ref, x_ref, gt_ref, w_ref, g_ref, b_ref, o_ref = refs
        z = z_ref[...].astype(BF16)
    y = _dot(z, w_ref[...])
    o_ref[...] = _layer_norm(DEEPNORM_ALPHA * x_ref[...] + (1.0 + gt_ref[0]) * y, g_ref[...], b_ref[...])


def out_projection_ln(z, x, mod, w, ln_g, ln_b, seq_len, *, zgate=None, tm=512):
    t_rows = x.shape[0]
    row = pl.BlockSpec((tm, D_MODEL), lambda i: (i, 0))
    vec = pl.BlockSpec((1, D_MODEL), lambda i: (0, 0))
    in_specs = [row] + ([row] if zgate is not None else []) + [
        row, _mod_spec(2, tm, seq_len), pl.BlockSpec((D_MODEL, D_MODEL), lambda i: (0, 0)), vec, vec]
    args = [z] + ([zgate] if zgate is not None else []) + [
        x, mod, w, ln_g.reshape(1, D_MODEL), ln_b.reshape(1, D_MODEL)]
    return pl.pallas_call(
        functools.partial(_out_ln_kernel, has_zgate=zgate is not None),
        grid=(t_rows // tm,),
        in_specs=in_specs,
        out_specs=row,
        out_shape=jax.ShapeDtypeStruct((t_rows, D_MODEL), F32),
        compiler_params=_params("parallel"),
        name="out_projection_ln",
    )(*args)


def _ffn_kernel(x_ref, sc_ref, sh_ref, gt_ref, wg_ref, wu_ref, wd_ref, g_ref, b_ref, o_ref, hb_ref, acc_ref):
    f = pl.program_id(1)

    @pl.when(f == 0)
    def _():
        hb_ref[...] = (x_ref[...] * (1.0 + sc_ref[0]) + sh_ref[0]).astype(BF16)
        acc_ref[...] = jnp.zeros_like(acc_ref)

    hb = hb_ref[...]
    gate = _dot(hb, wg_ref[...])
    up = _dot(hb, wu_ref[...])
    act = (gate * _sigmoid(gate) * up).astype(BF16)
    acc_ref[...] += _dot(act, wd_ref[...])

    @pl.when(f == pl.num_programs(1) - 1)
    def _():
        o_ref[...] = _layer_norm(DEEPNORM_ALPHA * x_ref[...] + (1.0 + gt_ref[0]) * acc_ref[...],
                                 g_ref[...], b_ref[...])


def swiglu_ln(x, mod, w_gu, w_down, ln_g, ln_b, seq_len, *, tm=512, tf=1408):
    t_rows = x.shape[0]
    nf = D_FF // tf
    row = pl.BlockSpec((tm, D_MODEL), lambda i, f: (i, 0))
    vec = pl.BlockSpec((1, D_MODEL), lambda i, f: (0, 0))
    return pl.pallas_call(
        _ffn_kernel,
        grid=(t_rows // tm, nf),
        in_specs=[row, _mod_spec(4, tm, seq_len), _mod_spec(3, tm, seq_len), _mod_spec(5, tm, seq_len),
                  pl.BlockSpec((D_MODEL, tf), lambda i, f: (0, f)),
                  pl.BlockSpec((D_MODEL, tf), lambda i, f: (0, nf + f)),
                  pl.BlockSpec((tf, D_MODEL), lambda i, f: (f, 0)),
                  vec, vec],
        out_specs=row,
        out_shape=jax.ShapeDtypeStruct((t_rows, D_MODEL), F32),
        scratch_shapes=[pltpu.VMEM((tm, D_MODEL), BF16), pltpu.VMEM((tm, D_MODEL), F32)],
        compiler_params=_params("parallel", "arbitrary"),
        name="swiglu_ln",
    )(x, mod, mod, mod, w_gu, w_gu, w_down, ln_g.reshape(1, D_MODEL), ln_b.reshape(1, D_MODEL))


def _moe_kernel(x_ref, sc_ref, sh_ref, gt_ref, r_ref, wg_ref, wu_ref, wd_ref, g_ref, b_ref, o_ref,
                hb_ref, comb_ref, acc_ref):
    e = pl.program_id(1)
    f = pl.program_id(2)

    @pl.when((e == 0) & (f == 0))
    def _():
        h = x_ref[...] * (1.0 + sc_ref[0]) + sh_ref[0]
        hb_ref[...] = h.astype(BF16)
        acc_ref[...] = jnp.zeros_like(acc_ref)
        logits = _dot_split(h, r_ref[...])
        lane = lax.broadcasted_iota(jnp.int32, logits.shape, 1)
        neg = jnp.float32(-jnp.inf)
        lg = jnp.where(lane < N_EXPERTS, logits, neg)
        m1 = jnp.max(lg, -1, keepdims=True)
        i1 = jnp.min(jnp.where(lg == m1, lane, LANES), -1, keepdims=True)
        lg2 = jnp.where(lane == i1, neg, lg)
        m2 = jnp.max(lg2, -1, keepdims=True)
        i2 = jnp.min(jnp.where(lg2 == m2, lane, LANES), -1, keepdims=True)
        e2 = jnp.exp(m2 - m1)
        g1 = 1.0 / (1.0 + e2)
        comb_ref[...] = jnp.where(lane == i1, g1, 0.0) + jnp.where(lane == i2, e2 * g1, 0.0)

    lane = lax.broadcasted_iota(jnp.int32, comb_ref.shape, 1)
    c_e = jnp.sum(jnp.where(lane == e, comb_ref[...], 0.0), -1, keepdims=True)

    @pl.when(jnp.max(c_e) > 0.0)
    def _():
        hb = hb_ref[...]
        gate = _dot(hb, wg_ref[...])
        up = _dot(hb, wu_ref[...])
        act = (gate * _sigmoid(gate) * up * c_e).astype(BF16)
        acc_ref[...] += _dot(act, wd_ref[...])

    @pl.when((e == pl.num_programs(1) - 1) & (f == pl.num_programs(2) - 1))
    def _():
        o_ref[...] = _layer_norm(DEEPNORM_ALPHA * x_ref[...] + (1.0 + gt_ref[0]) * acc_ref[...],
                                 g_ref[...], b_ref[...])


def moe_ln(x, mod, router, w_gu, w_down, ln_g, ln_b, seq_len, *, tm=1024, tf=512):
    t_rows = x.shape[0]
    nf = D_FF_EXPERT // tf
    router_p = jnp.pad(router, ((0, 0), (0, LANES - N_EXPERTS)))
    row = pl.BlockSpec((tm, D_MODEL), lambda i, e, f: (i, 0))
    vec = pl.BlockSpec((1, D_MODEL), lambda i, e, f: (0, 0))
    return pl.pallas_call(
        _moe_kernel,
        grid=(t_rows // tm, N_EXPERTS, nf),
        in_specs=[row, _mod_spec(4, tm, seq_len), _mod_spec(3, tm, seq_len), _mod_spec(5, tm, seq_len),
                  pl.BlockSpec((D_MODEL, LANES), lambda i, e, f: (0, 0)),
                  pl.BlockSpec((None, D_MODEL, tf), lambda i, e, f: (e, 0, f)),
                  pl.BlockSpec((None, D_MODEL, tf), lambda i, e, f: (e, 0, nf + f)),
                  pl.BlockSpec((None, tf, D_MODEL), lambda i, e, f: (e, f, 0)),
                  vec, vec],
        out_specs=row,
        out_shape=jax.ShapeDtypeStruct((t_rows, D_MODEL), F32),
        scratch_shapes=[pltpu.VMEM((tm, D_MODEL), BF16), pltpu.VMEM((tm, LANES), F32),
                        pltpu.VMEM((tm, D_MODEL), F32)],
        compiler_params=_params("parallel", "arbitrary", "arbitrary"),
        name="moe_ln",
    )(x, mod, mod, mod, router_p, w_gu, w_gu, w_down, ln_g.reshape(1, D_MODEL), ln_b.reshape(1, D_MODEL))


def _short_conv_kernel(u_ref, w_ref, b_ref, o_ref):
    u = u_ref[...]
    n = u.shape[0]
    row = lax.broadcasted_iota(jnp.int32, u.shape, 0)
    prev = jnp.where(row == 0, 0.0, pltpu.roll(u, 1, axis=0))
    nxt = jnp.where(row == n - 1, 0.0, pltpu.roll(u, n - 1, axis=0))
    o_ref[...] = prev * w_ref[0:1, :] + u * w_ref[1:2, :] + nxt * w_ref[2:3, :] + b_ref[...]


def short_conv(u, conv_w, conv_b, *, tn=256):
    nb, seq_len, width = u.shape
    blk = pl.BlockSpec((None, seq_len, tn), lambda b, j: (b, 0, j))
    return pl.pallas_call(
        _short_conv_kernel,
        grid=(nb, width // tn),
        in_specs=[blk, pl.BlockSpec((3, tn), lambda b, j: (0, j)), pl.BlockSpec((1, tn), lambda b, j: (0, j))],
        out_specs=blk,
        out_shape=jax.ShapeDtypeStruct(u.shape, F32),
        compiler_params=_params("parallel", "parallel"),
        name="short_conv",
    )(u, conv_w, conv_b.reshape(1, width))


def _filter_kernel(z_ref, w1_ref, b1_ref, w2_ref, b2_ref, w3_ref, b3_ref, w4_ref, fr_ref, dl_ref,
                   sum_ref, dif_ref, nrm_ref, nyq_ref):
    i = pl.program_id(0)
    z = z_ref[...]
    fr = fr_ref[...]
    h = jnp.sin(fr * (_dot_split(z, w1_ref[...]) + b1_ref[...]))
    h = jnp.sin(fr * (_dot_split(h, w2_ref[...]) + b2_ref[...]))
    h = jnp.sin(fr * (_dot_split(h, w3_ref[...]) + b3_ref[...]))
    hf = _dot_split(h, w4_ref[...])
    win = jnp.exp(-z[:, 0:1] * dl_ref[...])
    win = jnp.concatenate([win] * HYENA_ORDER, axis=1)
    half = HYENA_ORDER * D_MODEL
    row = lax.broadcasted_iota(jnp.int32, (z.shape[0], half), 0) + i * z.shape[0]
    kf = hf[:, :half] * win
    kb = jnp.where(row == 0, 0.0, hf[:, half:] * win)
    ksum = kf + kb
    sum_ref[...] = ksum.astype(sum_ref.dtype)
    dif_ref[...] = (kf - kb).astype(dif_ref.dtype)
    sign = (1 - 2 * (row % 2)).astype(F32)

    @pl.when(i == 0)
    def _():
        nrm_ref[...] = jnp.zeros_like(nrm_ref)
        nyq_ref[...] = jnp.zeros_like(nyq_ref)

    nrm_ref[...] += jnp.sum(jnp.abs(kf) + jnp.abs(kb), 0, keepdims=True)
    nyq_ref[...] += jnp.sum(ksum * sign, 0, keepdims=True)


def hyena_filter_taps(feat, w1p, b1, w2, b2, w3, b3, w4, freq, deltas, *, tl=256):
    seq_len = feat.shape[0]
    half = HYENA_ORDER * D_MODEL
    full = lambda a: pl.BlockSpec(a.shape, lambda i: (0,) * a.ndim)
    args = [feat, w1p, b1.reshape(1, -1), w2, b2.reshape(1, -1), w3, b3.reshape(1, -1), w4,
            freq.reshape(1, -1), deltas.reshape(1, -1)]
    return pl.pallas_call(
        _filter_kernel,
        grid=(seq_len // tl,),
        in_specs=[pl.BlockSpec((tl, LANES), lambda i: (i, 0))] + [full(a) for a in args[1:]],
        out_specs=[pl.BlockSpec((tl, half), lambda i: (i, 0)), pl.BlockSpec((tl, half), lambda i: (i, 0)),
                   pl.BlockSpec((1, half), lambda i: (0, 0)), pl.BlockSpec((1, half), lambda i: (0, 0))],
        out_shape=[jax.ShapeDtypeStruct((seq_len, half), BF16), jax.ShapeDtypeStruct((seq_len, half), BF16),
                   jax.ShapeDtypeStruct((1, half), F32), jax.ShapeDtypeStruct((1, half), F32)],
        compiler_params=_params("arbitrary"),
        name="hyena_filter_taps",
    )(*args)


def _spectrum_kernel(m_ref, k_ref, nrm_ref, nyq_ref, o_ref, acc_ref, *, patch_row0):
    kk = pl.program_id(2)

    @pl.when(kk == 0)
    def _():
        acc_ref[...] = jnp.zeros_like(acc_ref)

    acc_ref[...] += _dot(m_ref[...], k_ref[...])

    @pl.when(kk == pl.num_programs(2) - 1)
    def _():
        inv = 1.0 / (nrm_ref[...] + FILTER_NORM_EPS)
        out = acc_ref[...] * inv
        if patch_row0:
            row = lax.broadcasted_iota(jnp.int32, out.shape, 0) + pl.program_id(0) * out.shape[0]
            out = jnp.where(row == 0, nyq_ref[...] * inv, out)
        o_ref[...] = out


def filter_spectrum(mat, taps, nrm, nyq, *, patch_row0, tm=512, tn=512, tk=512):
    seq_len, width = taps.shape
    return pl.pallas_call(
        functools.partial(_spectrum_kernel, patch_row0=patch_row0),
        grid=(seq_len // tm, width // tn, seq_len // tk),
        in_specs=[pl.BlockSpec((tm, tk), lambda i, j, k: (i, k)),
                  pl.BlockSpec((tk, tn), lambda i, j, k: (k, j)),
                  pl.BlockSpec((1, tn), lambda i, j, k: (0, j)),
                  pl.BlockSpec((1, tn), lambda i, j, k: (0, j))],
        out_specs=pl.BlockSpec((tm, tn), lambda i, j, k: (i, j)),
        out_shape=jax.ShapeDtypeStruct((seq_len, width), F32),
        scratch_shapes=[pltpu.VMEM((tm, tn), F32)],
        compiler_params=_params("parallel", "parallel", "arbitrary"),
        name="filter_spectrum",
    )(mat, taps, nrm, nyq)


def _long_conv_kernel(z_ref, gate_ref, hr_ref, hi_ref, skip_ref, fc_ref, fs_ref, gc_ref, gs_ref, o_ref,
                      zb_ref, acc_ref):
    f = pl.program_id(2)

    @pl.when(f == 0)
    def _():
        zb_ref[...] = z_ref[...].astype(BF16)
        acc_ref[...] = jnp.zeros_like(acc_ref)

    zb = zb_ref[...]
    xr = _dot(fc_ref[...], zb)
    xi = _dot(fs_ref[...], zb)
    hr = hr_ref[...]
    hi = hi_ref[...]
    row = lax.broadcasted_iota(jnp.int32, xr.shape, 0)
    packed = (row == 0) & (f == 0)
    pr = jnp.where(packed, xr * hr, xr * hr - xi * hi)
    pi = jnp.where(packed, xi * hi, xr * hi + xi * hr)
    acc_ref[...] += _dot(gc_ref[...], pr.astype(BF16)) + _dot(gs_ref[...], pi.astype(BF16))

    @pl.when(f == pl.num_programs(2) - 1)
    def _():
        conv = acc_ref[...] * (1.0 / z_ref.shape[0])
        o_ref[...] = gate_ref[...] * (conv + z_ref[...] * skip_ref[...])


def long_conv_gate(zsrc, z_col0, gsrc, g_col0, spec_r, spec_i, order, skip, mats, *, tn, tf):
    nb, seq_len, _ = zsrc.shape
    fc, fs, gc, gs = mats
    nct = D_MODEL // tn
    return pl.pallas_call(
        _long_conv_kernel,
        grid=(nb, nct, seq_len // tf),
        in_specs=[pl.BlockSpec((None, seq_len, tn), lambda b, c, f: (b, 0, z_col0 + c)),
                  pl.BlockSpec((None, seq_len, tn), lambda b, c, f: (b, 0, g_col0 + c)),
                  pl.BlockSpec((tf, tn), lambda b, c, f: (f, order * nct + c)),
                  pl.BlockSpec((tf, tn), lambda b, c, f: (f, order * nct + c)),
                  pl.BlockSpec((None, 1, tn), lambda b, c, f: (order, 0, c)),
                  pl.BlockSpec((tf, seq_len), lambda b, c, f: (f, 0)),
                  pl.BlockSpec((tf, seq_len), lambda b, c, f: (f, 0)),
                  pl.BlockSpec((seq_len, tf), lambda b, c, f: (0, f)),
                  pl.BlockSpec((seq_len, tf), lambda b, c, f: (0, f))],
        out_specs=pl.BlockSpec((None, seq_len, tn), lambda b, c, f: (b, 0, c)),
        out_shape=jax.ShapeDtypeStruct((nb, seq_len, D_MODEL), F32),
        scratch_shapes=[pltpu.VMEM((seq_len, tn), BF16), pltpu.VMEM((seq_len, tn), F32)],
        compiler_params=_params("parallel", "parallel", "arbitrary"),
        name="long_conv_gate",
    )(zsrc, gsrc, spec_r, spec_i, skip.reshape(HYENA_ORDER, 1, D_MODEL), fc, fs, gc, gs)


def dft_matrices(seq_len):
    idx = jnp.arange(seq_len, dtype=jnp.int32)
    m = (idx[:, None] * idx[None, :]) % (2 * seq_len)
    ang = m.astype(F32) * (math.pi / seq_len)
    cosm = jnp.cos(ang)
    sinm = -jnp.sin(ang)
    alt = (1 - 2 * (idx % 2)).astype(F32)
    fc = cosm
    fs = sinm.at[0, :].set(alt)
    gc = cosm.at[:, 0].set(0.5)
    gs = sinm.at[:, 0].set(0.5 * alt)
    return tuple(a.astype(BF16) for a in (fc, fs, gc, gs))


def filter_features(seq_len):
    t = jnp.linspace(0.0, 1.0, seq_len, dtype=F32)[:, None]
    omega = 2.0 * math.pi * jnp.arange(seq_len, dtype=F32)[:, None] / seq_len
    bands = jnp.linspace(1e-4, FILTER_BANDS - 1, FILTER_BANDS, dtype=F32)[None, :]
    z = jnp.concatenate([t, jnp.cos(bands * omega), -jnp.sin(bands * omega)], -1)
    return jnp.pad(z, ((0, 0), (0, LANES - z.shape[1])))


def hyena_mixer(x, mod, hp, nb, seq_len):
    tn = 512 if seq_len <= 2048 else 256
    u = modulated_projection(x, mod, hp["w_in"], seq_len)
    u = short_conv(u.reshape(nb, seq_len, 3 * D_MODEL), hp["conv_w"], hp["conv_b"])
    mats = dft_matrices(seq_len)
    deltas = jnp.abs(jnp.linspace(HYENA_MIN_DECAY, HYENA_MAX_DECAY, D_MODEL, dtype=F32))
    ksum, kdif, nrm, nyq = hyena_filter_taps(filter_features(seq_len), hp["f_w1p"], hp["f_b1"], hp["f_w2"],
                                             hp["f_b2"], hp["f_w3"], hp["f_b3"], hp["f_w4"], hp["f_freq"], deltas)
    spec_r = filter_spectrum(mats[0], ksum, nrm, nyq, patch_row0=False)
    spec_i = filter_spectrum(mats[1], kdif, nrm, nyq, patch_row0=True)
    nct = D_MODEL // tn
    z = long_conv_gate(u, 0, u, nct, spec_r, spec_i, 0, hp["skip"], mats, tn=tn, tf=256)
    z = long_conv_gate(z, 0, u, 2 * nct, spec_r, spec_i, 1, hp["skip"], mats, tn=tn, tf=256)
    return z.reshape(nb * seq_len, D_MODEL)


def _rwkv_prep_kernel(x_ref, xp_ref, xn_ref, sc_ref, sh_ref, mu_ref, wrkv_ref, w1_ref, w2_ref, a1_ref, a2_ref,
                      g1_ref, g2_ref, w0_ref, a0_ref,
                      r_ref, k_ref, v_ref, dec0_ref, dec1_ref, as0_ref, as1_ref, g_ref, *, seq_len):
    i = pl.program_id(0)
    tm = x_ref.shape[0]
    sc = 1.0 + sc_ref[0]
    sh = sh_ref[0]
    h = x_ref[...] * sc + sh
    row = lax.broadcasted_iota(jnp.int32, h.shape, 0)
    pos = (row + i * tm) % seq_len
    halo_prev = xp_ref[7:8, :] * sc + sh
    halo_next = xn_ref[0:1, :] * sc + sh
    prev = jnp.where(row == 0, halo_prev, pltpu.roll(h, 1, axis=0))
    prev = jnp.where(pos == 0, 0.0, prev)
    nxt = jnp.where(row == tm - 1, halo_next, pltpu.roll(h, tm - 1, axis=0))
    nxt = jnp.where(pos == seq_len - 1, 0.0, nxt)
    xx = 0.5 * (prev + nxt) - h

    def mix(j):
        return (h + xx * mu_ref[j:j + 1, :]).astype(BF16)

    r_ref[...] = _dot(mix(0), wrkv_ref[0])
    k_ref[...] = _dot(mix(1), wrkv_ref[1])
    v_ref[...] = _dot(mix(2), wrkv_ref[2])
    lw = _dot(jnp.tanh(_dot(mix(3), w1_ref[...])).astype(BF16), w2_ref[...])
    la = _dot(_dot(mix(4), a1_ref[...]).astype(BF16), a2_ref[...])
    g_ref[...] = _dot(_sigmoid(_dot(mix(5), g1_ref[...])).astype(BF16), g2_ref[...])
    for d, (dec_ref, as_ref) in enumerate(((dec0_ref, as0_ref), (dec1_ref, as1_ref))):
        cols = slice(d * D_MODEL, (d + 1) * D_MODEL)
        pre = -(w0_ref[d:d + 1, :] + lw[:, cols])
        softplus = jnp.maximum(pre, 0.0) + jnp.log(1.0 + jnp.exp(-jnp.abs(pre)))
        dec_ref[...] = jnp.exp(-jnp.exp(-softplus - 0.5))
        as_ref[...] = _sigmoid(a0_ref[d:d + 1, :] + la[:, cols])


def rwkv_prep(x, mod, rp, seq_len, *, tm=256):
    t_rows = x.shape[0]
    row = pl.BlockSpec((tm, D_MODEL), lambda i: (i, 0))
    halo = tm // 8
    last8 = t_rows // 8 - 1
    full = lambda a: pl.BlockSpec(a.shape, lambda i: (0,) * a.ndim)
    weights = [rp["mu"], rp["w_rkv"], rp["w1"], rp["w2"], rp["a1"], rp["a2"], rp["g1"], rp["g2"], rp["w0"], rp["a0"]]
    return pl.pallas_call(
        functools.partial(_rwkv_prep_kernel, seq_len=seq_len),
        grid=(t_rows // tm,),
        in_specs=[row,
                  pl.BlockSpec((8, D_MODEL), lambda i: (jnp.maximum(i * halo - 1, 0), 0)),
                  pl.BlockSpec((8, D_MODEL), lambda i: (jnp.minimum((i + 1) * halo, last8), 0)),
                  _mod_spec(1, tm, seq_len), _mod_spec(0, tm, seq_len)] + [full(a) for a in weights],
        out_specs=[row] * 8,
        out_shape=[jax.ShapeDtypeStruct((t_rows, D_MODEL), F32)] * 8,
        compiler_params=_params("parallel"),
        name="rwkv_prep",
    )(x, x, x, mod, mod, *weights)


def _wkv_scan_kernel(*refs, reverse, tt):
    if reverse:
        (r_ref, k_ref, v_ref, w_ref, as_ref, kkp_ref, kap_ref, rkp_ref, gng_ref, gnb_ref, yin_ref, bin_ref,
         y_ref, s_ref, na_ref, b_ref, kd_ref) = refs
    else:
        (r_ref, k_ref, v_ref, w_ref, as_ref, kkp_ref, kap_ref, rkp_ref,
         y_ref, bout_ref, s_ref, na_ref, b_ref, kd_ref) = refs
    n = RWKV_HEAD

    @pl.when(pl.program_id(1) == 0)
    def _():
        s_ref[...] = jnp.zeros_like(s_ref)

    r = r_ref[...]
    k = k_ref[...]
    a_sig = as_ref[...]
    kk = k * kkp_ref[...][None]
    kk = kk / jnp.maximum(jnp.sqrt(jnp.sum(kk * kk, 1, keepdims=True)), 1e-12)
    kd = k * (1.0 + (a_sig - 1.0) * kap_ref[...][None])
    na_ref[...] = -kk
    b_ref[...] = kk * a_sig
    kd_ref[...] = kd
    bonus = jnp.sum(r * kd * rkp_ref[...][None], 1, keepdims=True)

    def step(j, carry):
        t = tt - 1 - j if reverse else j
        a_t = na_ref[t]
        w_t = w_ref[t]
        b_t = b_ref[t]
        kd_t = kd_ref[t]
        r_t = r_ref[t]
        v_t = v_ref[t]
        sa = jnp.zeros((n, LANES), F32)
        for kc in range(n):
            sa = sa + s_ref[kc] * a_t[kc:kc + 1, :]
        y = jnp.zeros((n, LANES), F32)
        for kc in range(n):
            s_new = s_ref[kc] * w_t[kc:kc + 1, :] + sa * b_t[kc:kc + 1, :] + v_t * kd_t[kc:kc + 1, :]
            s_ref[kc] = s_new
            y = y + s_new * r_t[kc:kc + 1, :]
        y_ref[t] = y
        return carry

    lax.fori_loop(0, tt, step, 0)

    if reverse:
        y = y_ref[...] + yin_ref[...]
        mean = jnp.mean(y, 1, keepdims=True)
        yc = y - mean
        var = jnp.mean(yc * yc, 1, keepdims=True)
        yn = yc * lax.rsqrt(var + RWKV_GN_EPS) * gng_ref[...][None] + gnb_ref[...][None]
        y_ref[...] = yn + (bonus + bin_ref[...]) * v_ref[...]
    else:
        bout_ref[...] = bonus


def wkv_scan(r, k, v, w, a_sig, chan, *, reverse, y_in=None, bonus_in=None, tt=32):
    seq_len, n, chains = r.shape
    nt = seq_len // tt
    tmap = (lambda c, j: (nt - 1 - j, 0, c)) if reverse else (lambda c, j: (j, 0, c))
    blk = pl.BlockSpec((tt, n, LANES), tmap)
    bblk = pl.BlockSpec((tt, 1, LANES), tmap)
    par = pl.BlockSpec((n, LANES), lambda c, j: (0, 0))
    in_specs = [blk] * 5 + [par] * 3
    args = [r, k, v, w, a_sig, chan["k_k"], chan["k_a"], chan["r_k"]]
    if reverse:
        in_specs += [par, par, blk, bblk]
        args += [chan["gn_g"], chan["gn_b"], y_in, bonus_in]
        out_specs = blk
        out_shape = jax.ShapeDtypeStruct(r.shape, F32)
    else:
        out_specs = [blk, bblk]
        out_shape = [jax.ShapeDtypeStruct(r.shape, F32), jax.ShapeDtypeStruct((seq_len, 1, chains), F32)]
    return pl.pallas_call(
        functools.partial(_wkv_scan_kernel, reverse=reverse, tt=tt),
        grid=(chains // LANES, nt),
        in_specs=in_specs,
        out_specs=out_specs,
        out_shape=out_shape,
        scratch_shapes=[pltpu.VMEM((n, n, LANES), F32)] + [pltpu.VMEM((tt, n, LANES), F32)] * 3,
        compiler_params=_params("parallel", "arbitrary"),
        name="wkv_scan_rev" if reverse else "wkv_scan_fwd",
    )(*args)


def rwkv_mixer(x, mod, rp, nb, seq_len):
    r, k, v, dec0, dec1, as0, as1, g = rwkv_prep(x, mod, rp, seq_len)
    chains = nb * RWKV_HEADS

    def to_chains(a):
        return a.reshape(nb, seq_len, RWKV_HEADS, RWKV_HEAD).transpose(1, 3, 0, 2).reshape(seq_len, RWKV_HEAD, chains)

    rt, kt, vt = to_chains(r), to_chains(k), to_chains(v)
    y_f, bonus_f = wkv_scan(rt, kt, vt, to_chains(dec0), to_chains(as0), rp["chan"][0], reverse=False)
    y = wkv_scan(rt, kt, vt, to_chains(dec1), to_chains(as1), rp["chan"][1], reverse=True,
                 y_in=y_f, bonus_in=bonus_f)
    y = y.reshape(seq_len, RWKV_HEAD, nb, RWKV_HEADS).transpose(2, 0, 3, 1).reshape(nb * seq_len, D_MODEL)
    return y, g


def _diff_attn_kernel(lam_ref, g_ref, q_ref, k_ref, v_ref, o_ref, *, lam_init):
    lf = lam_ref[...]
    lam = (jnp.exp(jnp.sum(lf[0:1] * lf[1:2], -1, keepdims=True))
           - jnp.exp(jnp.sum(lf[2:3] * lf[3:4], -1, keepdims=True)) + lam_init)
    q = q_ref[...]
    k = k_ref[...]

    def softmax_terms(cols):
        s = lax.dot_general(q[:, cols], k[:, cols], (((1,), (1,)), ((), ())), preferred_element_type=F32)
        e = jnp.exp(s - jnp.max(s, -1, keepdims=True))
        return e, 1.0 / jnp.sum(e, -1, keepdims=True)

    e1, inv1 = softmax_terms(slice(0, DA_HEAD))
    e2, inv2 = softmax_terms(slice(DA_HEAD, 2 * DA_HEAD))
    a = e1 * inv1 - e2 * (lam * inv2)
    o = _dot(a.astype(BF16), v_ref[...])
    o = o * lax.rsqrt(jnp.mean(o * o, -1, keepdims=True) + SUBLN_EPS) * g_ref[...] * (1.0 - lam_init)
    o_ref[...] = o.astype(o_ref.dtype)


def diff_attention(qkv, lam, subln_g, layer_idx, nb, seq_len, *, tq=256):
    lam_init = 0.8 - 0.6 * math.exp(-0.3 * layer_idx)
    hd = 2 * DA_HEAD
    nq = seq_len // tq
    return pl.pallas_call(
        functools.partial(_diff_attn_kernel, lam_init=lam_init),
        grid=(nb, DA_HEADS, nq),
        in_specs=[pl.BlockSpec((4, DA_HEAD), lambda b, h, i: (0, 0)),
                  pl.BlockSpec((1, hd), lambda b, h, i: (0, 0)),
                  pl.BlockSpec((tq, hd), lambda b, h, i: (b * nq + i, h)),
                  pl.BlockSpec((seq_len, hd), lambda b, h, i: (b, DA_HEADS + h)),
                  pl.BlockSpec((seq_len, hd), lambda b, h, i: (b, 2 * DA_HEADS + h))],
        out_specs=pl.BlockSpec((tq, hd), lambda b, h, i: (b * nq + i, h)),
        out_shape=jax.ShapeDtypeStruct((nb * seq_len, D_MODEL), BF16),
        compiler_params=_params("parallel", "parallel", "arbitrary"),
        name="diff_attention",
    )(lam, subln_g.reshape(1, hd), qkv, qkv, qkv)


def _chan_table(p):
    t = p.reshape(RWKV_HEADS, RWKV_HEAD).T
    return jnp.tile(t, (1, LANES // RWKV_HEADS))


def _prepare_weights(P):
    bf = lambda a: a.astype(BF16)
    W = {"hy": [], "rw": [], "da": []}
    for j in range(P["hy_w_in"].shape[0]):
        W["hy"].append(dict(
            w_in=bf(P["hy_w_in"][j]), conv_w=P["hy_conv_w"][j], conv_b=P["hy_conv_b"][j],
            f_w1p=jnp.pad(P["hy_f_w1"][j], ((0, LANES - P["hy_f_w1"].shape[1]), (0, 0))),
            f_b1=P["hy_f_b1"][j], f_w2=P["hy_f_w2"][j], f_b2=P["hy_f_b2"][j], f_w3=P["hy_f_w3"][j],
            f_b3=P["hy_f_b3"][j], f_w4=P["hy_f_w4"][j], f_freq=P["hy_f_freq"][j], skip=P["hy_bias"][j],
            w_out=bf(P["hy_w_out"][j])))
    for j in range(P["rw_w_rkv"].shape[0]):
        zero = jnp.zeros((RWKV_LORA, D_MODEL), F32)
        blockdiag = lambda m: jnp.concatenate([jnp.concatenate([m[0], zero], 1), jnp.concatenate([zero, m[1]], 1)], 0)
        gpad = 2 * LANES - GATE_LORA
        W["rw"].append(dict(
            mu=P["rw_mu"][j], w_rkv=bf(P["rw_w_rkv"][j]),
            w1=bf(jnp.concatenate([P["rw_w1"][j, 0], P["rw_w1"][j, 1]], 1)), w2=bf(blockdiag(P["rw_w2"][j])),
            a1=bf(jnp.concatenate([P["rw_a1"][j, 0], P["rw_a1"][j, 1]], 1)), a2=bf(blockdiag(P["rw_a2"][j])),
            g1=bf(jnp.pad(P["rw_g1"][j], ((0, 0), (0, gpad)))), g2=bf(jnp.pad(P["rw_g2"][j], ((0, gpad), (0, 0)))),
            w0=P["rw_w0"][j], a0=P["rw_a0"][j],
            chan=[dict(k_k=_chan_table(P["rw_k_k"][j]), k_a=_chan_table(P["rw_k_a"][j]),
                       r_k=_chan_table(P["rw_r_k"][j, d]), gn_g=_chan_table(P["rw_gn_g"][j]),
                       gn_b=_chan_table(P["rw_gn_b"][j])) for d in range(2)],
            w_out=bf(P["rw_w_out"][j])))
    for j in range(P["da_w_qkv"].shape[0]):
        W["da"].append(dict(w_qkv=bf(P["da_w_qkv"][j]), lam=P["da_lam"][j], subln_g=P["da_subln_g"][j],
                            w_out=bf(P["da_w_out"][j])))
    W["ff_w_gu"], W["ff_w_down"] = bf(P["ff_w_gu"]), bf(P["ff_w_down"])
    W["moe_w_gu"], W["moe_w_down"] = bf(P["moe_w_gu"]), bf(P["moe_w_down"])
    return W


def _trunk(x, mods, P, W):
    nb, seq_len, _ = x.shape
    x = x.reshape(nb * seq_len, D_MODEL)
    for i in range(DEPTH):
        mod = mods[i].reshape(nb * 6, 1, D_MODEL)
        kind, j = i % N_MIXERS, i // N_MIXERS
        zgate = None
        if kind == 0:
            z = hyena_mixer(x, mod, W["hy"][j], nb, seq_len)
            w_out = W["hy"][j]["w_out"]
        elif kind == 1:
            z, zgate = rwkv_mixer(x, mod, W["rw"][j], nb, seq_len)
            w_out = W["rw"][j]["w_out"]
        else:
            da = W["da"][j]
            qkv = qkv_rope_projection(x, mod, da["w_qkv"], seq_len)
            z = diff_attention(qkv, da["lam"], da["subln_g"], i, nb, seq_len)
            w_out = da["w_out"]
        x = out_projection_ln(z, x, mod, w_out, P["ln_g"][i, 0], P["ln_b"][i, 0], seq_len, zgate=zgate)
        if i % 2 == 0:
            x = swiglu_ln(x, mod, W["ff_w_gu"][i // 2], W["ff_w_down"][i // 2],
                          P["ln_g"][i, 1], P["ln_b"][i, 1], seq_len)
        else:
            x = moe_ln(x, mod, P["moe_router"][i // 2], W["moe_w_gu"][i // 2], W["moe_w_down"][i // 2],
                       P["ln_g"][i, 1], P["ln_b"][i, 1], seq_len)
    return x.reshape(nb, seq_len, D_MODEL)


def kernel(x_prompt, x_sample, c_prompt, c_sample, ada_w, ada_b, ln_g, ln_b, hy_w_in, hy_conv_w, hy_conv_b, hy_f_w1, hy_f_b1, hy_f_w2, hy_f_b2, hy_f_w3, hy_f_b3, hy_f_w4, hy_f_freq, hy_bias, hy_w_out, rw_mu, rw_w_rkv, rw_w0, rw_w1, rw_w2, rw_a0, rw_a1, rw_a2, rw_g1, rw_g2, rw_k_k, rw_k_a, rw_r_k, rw_gn_g, rw_gn_b, rw_w_out, da_w_qkv, da_lam, da_subln_g, da_w_out, ff_w_gu, ff_w_down, moe_router, moe_w_gu, moe_w_down):
    P = dict(ada_w=ada_w, ada_b=ada_b, ln_g=ln_g, ln_b=ln_b,
             hy_w_in=hy_w_in, hy_conv_w=hy_conv_w, hy_conv_b=hy_conv_b,
             hy_f_w1=hy_f_w1, hy_f_b1=hy_f_b1, hy_f_w2=hy_f_w2, hy_f_b2=hy_f_b2,
             hy_f_w3=hy_f_w3, hy_f_b3=hy_f_b3, hy_f_w4=hy_f_w4, hy_f_freq=hy_f_freq,
             hy_bias=hy_bias, hy_w_out=hy_w_out,
             rw_mu=rw_mu, rw_w_rkv=rw_w_rkv, rw_w0=rw_w0, rw_w1=rw_w1, rw_w2=rw_w2,
             rw_a0=rw_a0, rw_a1=rw_a1, rw_a2=rw_a2, rw_g1=rw_g1, rw_g2=rw_g2,
             rw_k_k=rw_k_k, rw_k_a=rw_k_a, rw_r_k=rw_r_k, rw_gn_g=rw_gn_g, rw_gn_b=rw_gn_b,
             rw_w_out=rw_w_out,
             da_w_qkv=da_w_qkv, da_lam=da_lam, da_subln_g=da_subln_g, da_w_out=da_w_out,
             ff_w_gu=ff_w_gu, ff_w_down=ff_w_down,
             moe_router=moe_router, moe_w_gu=moe_w_gu, moe_w_down=moe_w_down)
    W = _prepare_weights(P)
    n_prompt = x_prompt.shape[0]
    mods = ada_modulation(jnp.concatenate([c_prompt, c_sample], 0), ada_w, ada_b)
    y_prompt = _trunk(x_prompt, mods[:, :n_prompt], P, W)
    y_sample = _trunk(x_sample, mods[:, n_prompt:], P, W)
    return (y_prompt, y_sample)
```

```python
import functools
import math

import jax
import jax.numpy as jnp
from jax import lax
from jax.experimental import pallas as pl
from jax.experimental.pallas import tpu as pltpu

F32 = jnp.float32
BF16 = jnp.bfloat16

D_MODEL = 1024
DEPTH = 4
N_MIXERS = 3
DEEPNORM_ALPHA = (2 * DEPTH) ** 0.25
LN_EPS = 1e-5

HYENA_ORDER = 2
FILTER_BANDS = 16
FILTER_HIDDEN = 64
HYENA_MIN_DECAY = math.log(1e-2) / 1.5
HYENA_MAX_DECAY = math.log(1e-2) / 0.3
FILTER_NORM_EPS = 1e-6

RWKV_HEAD = 64
RWKV_HEADS = D_MODEL // RWKV_HEAD
RWKV_LORA = 64
GATE_LORA = 160
RWKV_GN_EPS = 64e-5

DA_HEAD = 64
DA_HEADS = D_MODEL // (2 * DA_HEAD)
ROPE_THETA = 10000.0
SUBLN_EPS = 1e-5

D_FF = 2816
N_EXPERTS = 8
D_FF_EXPERT = 3584

LANES = 128
VMEM_LIMIT_BYTES = 56 * 1024 * 1024


def _params(*semantics):
    return pltpu.CompilerParams(dimension_semantics=semantics, vmem_limit_bytes=VMEM_LIMIT_BYTES)


def _split_bf16(x):
    hi = x.astype(BF16)
    lo = (x - hi.astype(F32)).astype(BF16)
    return hi, lo


def _dot(a, b):
    return jnp.dot(a, b, preferred_element_type=F32)


def _dot_split(a, b):
    ah, al = _split_bf16(a)
    bh, bl = _split_bf16(b)
    return _dot(ah, bh) + _dot(al, bh) + _dot(ah, bl)


def _sigmoid(x):
    return 1.0 / (1.0 + jnp.exp(-x))


def _layer_norm(x, g, b):
    mu = jnp.mean(x, -1, keepdims=True)
    xc = x - mu
    var = jnp.mean(xc * xc, -1, keepdims=True)
    return xc * lax.rsqrt(var + LN_EPS) * g + b


def _mod_spec(which, tm, seq_len):
    return pl.BlockSpec((1, 1, D_MODEL), lambda i, *_: ((i * tm // seq_len) * 6 + which, 0, 0))


def _ada_kernel(c_ref, w_ref, b_ref, o_ref):
    c = c_ref[...]
    cs = (c * _sigmoid(c)).astype(BF16)
    o_ref[...] = _dot(cs, w_ref[...].astype(BF16)) + b_ref[...]


def ada_modulation(c, ada_w, ada_b):
    nb = c.shape[0]
    tn = 1024
    return pl.pallas_call(
        _ada_kernel,
        grid=(DEPTH, 6 * D_MODEL // tn),
        in_specs=[pl.BlockSpec((nb, D_MODEL), lambda l, j: (0, 0)),
                  pl.BlockSpec((None, D_MODEL, tn), lambda l, j: (l, 0, j)),
                  pl.BlockSpec((None, 1, tn), lambda l, j: (l, 0, j))],
        out_specs=pl.BlockSpec((None, nb, tn), lambda l, j: (l, 0, j)),
        out_shape=jax.ShapeDtypeStruct((DEPTH, nb, 6 * D_MODEL), F32),
        compiler_params=_params("parallel", "parallel"),
        name="ada_modulation",
    )(c, ada_w, ada_b.reshape(DEPTH, 1, 6 * D_MODEL))


def _proj_kernel(x_ref, sc_ref, sh_ref, w_ref, o_ref, xb_ref):
    @pl.when(pl.program_id(1) == 0)
    def _():
        xb_ref[...] = (x_ref[...] * (1.0 + sc_ref[0]) + sh_ref[0]).astype(BF16)

    o_ref[...] = _dot(xb_ref[...], w_ref[...]).astype(o_ref.dtype)


def _qkv_rope_kernel(x_ref, sc_ref, sh_ref, w_ref, cos_ref, sin_ref, o_ref, xb_ref, *, n_rope_tiles):
    j = pl.program_id(1)

    @pl.when(j == 0)
    def _():
        xb_ref[...] = (x_ref[...] * (1.0 + sc_ref[0]) + sh_ref[0]).astype(BF16)

    y = _dot(xb_ref[...], w_ref[...])
    tn = y.shape[1]

    @pl.when(j < n_rope_tiles)
    def _():
        lane = lax.broadcasted_iota(jnp.int32, y.shape, 1)
        first_half = (lane % DA_HEAD) < (DA_HEAD // 2)
        partner = jnp.where(first_half,
                            pltpu.roll(y, tn - DA_HEAD // 2, axis=1),
                            pltpu.roll(y, DA_HEAD // 2, axis=1))
        o_ref[...] = (y * cos_ref[...] + partner * sin_ref[...]).astype(o_ref.dtype)

    @pl.when(j >= n_rope_tiles)
    def _():
        o_ref[...] = y.astype(o_ref.dtype)


def modulated_projection(x, mod, w, seq_len, *, tm=512, tn=512):
    t_rows, n_out = x.shape[0], w.shape[1]
    return pl.pallas_call(
        _proj_kernel,
        grid=(t_rows // tm, n_out // tn),
        in_specs=[pl.BlockSpec((tm, D_MODEL), lambda i, j: (i, 0)),
                  _mod_spec(1, tm, seq_len), _mod_spec(0, tm, seq_len),
                  pl.BlockSpec((D_MODEL, tn), lambda i, j: (0, j))],
        out_specs=pl.BlockSpec((tm, tn), lambda i, j: (i, j)),
        out_shape=jax.ShapeDtypeStruct((t_rows, n_out), F32),
        scratch_shapes=[pltpu.VMEM((tm, D_MODEL), BF16)],
        compiler_params=_params("parallel", "arbitrary"),
        name="modulated_projection",
    )(x, mod, mod, w)


def qkv_rope_projection(x, mod, w, seq_len, *, tm=512, tn=512):
    t_rows = x.shape[0]
    tiles_per_seq = seq_len // tm
    nq_tiles = D_MODEL // tn
    half = DA_HEAD // 2
    inv = ROPE_THETA ** (-jnp.arange(0, DA_HEAD, 2, dtype=F32) / DA_HEAD)
    ang = jnp.arange(seq_len, dtype=F32)[:, None] * inv[None, :]
    cos_t = jnp.tile(jnp.concatenate([jnp.cos(ang), jnp.cos(ang)], -1), (1, tn // DA_HEAD))
    sin_t = jnp.tile(jnp.concatenate([-jnp.sin(ang), jnp.sin(ang)], -1), (1, tn // DA_HEAD))
    qscale = jnp.float32(DA_HEAD ** -0.5)
    cos_all = jnp.concatenate([cos_t * qscale, cos_t], 0)
    sin_all = jnp.concatenate([sin_t * qscale, sin_t], 0)
    tab = lambda i, j: (jnp.where(j < nq_tiles, 0, tiles_per_seq) + i % tiles_per_seq, 0)
    return pl.pallas_call(
        functools.partial(_qkv_rope_kernel, n_rope_tiles=2 * nq_tiles),
        grid=(t_rows // tm, 3 * D_MODEL // tn),
        in_specs=[pl.BlockSpec((tm, D_MODEL), lambda i, j: (i, 0)),
                  _mod_spec(1, tm, seq_len), _mod_spec(0, tm, seq_len),
                  pl.BlockSpec((D_MODEL, tn), lambda i, j: (0, j)),
                  pl.BlockSpec((tm, tn), tab), pl.BlockSpec((tm, tn), tab)],
        out_specs=pl.BlockSpec((tm, tn), lambda i, j: (i, j)),
        out_shape=jax.ShapeDtypeStruct((t_rows, 3 * D_MODEL), BF16),
        scratch_shapes=[pltpu.VMEM((tm, D_MODEL), BF16)],
        compiler_params=_params("parallel", "arbitrary"),
        name="qkv_rope_projection",
    )(x, mod, mod, w, cos_all, sin_all)


def _out_ln_kernel(*refs, has_zgate):
    if has_zgate:
        z_ref, zg_ref, x_ref, gt_ref, w_ref, g_ref, b_ref, o_ref = refs
        z = (z_ref[...].astype(F32) * zg_ref[...]).astype(BF16)
    else:
        z_ref, x_ref, gt_ref, w_ref, g_ref, b_ref, o_ref = refs
        z = z_ref[...].astype(BF16)
    y = _dot(z, w_ref[...])
    o_ref[...] = _layer_norm(DEEPNORM_ALPHA * x_ref[...] + (1.0 + gt_ref[0]) * y, g_ref[...], b_ref[...])


def out_projection_ln(z, x, mod, w, ln_g, ln_b, seq_len, *, zgate=None, tm=512):
    t_rows = x.shape[0]
    row = pl.BlockSpec((tm, D_MODEL), lambda i: (i, 0))
    vec = pl.BlockSpec((1, D_MODEL), lambda i: (0, 0))
    in_specs = [row] + ([row] if zgate is not None else []) + [
        row, _mod_spec(2, tm, seq_len), pl.BlockSpec((D_MODEL, D_MODEL), lambda i: (0, 0)), vec, vec]
    args = [z] + ([zgate] if zgate is not None else []) + [
        x, mod, w, ln_g.reshape(1, D_MODEL), ln_b.reshape(1, D_MODEL)]
    return pl.pallas_call(
        functools.partial(_out_ln_kernel, has_zgate=zgate is not None),
        grid=(t_rows // tm,),
        in_specs=in_specs,
        out_specs=row,
        out_shape=jax.ShapeDtypeStruct((t_rows, D_MODEL), F32),
        compiler_params=_params("parallel"),
        name="out_projection_ln",
    )(*args)


def _ffn_kernel(x_ref, sc_ref, sh_ref, gt_ref, wg_ref, wu_ref, wd_ref, g_ref, b_ref, o_ref, hb_ref, acc_ref):
    f = pl.program_id(1)

    @pl.when(f == 0)
    def _():
        hb_ref[...] = (x_ref[...] * (1.0 + sc_ref[0]) + sh_ref[0]).astype(BF16)
        acc_ref[...] = jnp.zeros_like(acc_ref)

    hb = hb_ref[...]
    gate = _dot(hb, wg_ref[...])
    up = _dot(hb, wu_ref[...])
    act = (gate * _sigmoid(gate) * up).astype(BF16)
    acc_ref[...] += _dot(act, wd_ref[...])

    @pl.when(f == pl.num_programs(1) - 1)
    def _():
        o_ref[...] = _layer_norm(DEEPNORM_ALPHA * x_ref[...] + (1.0 + gt_ref[0]) * acc_ref[...],
                                 g_ref[...], b_ref[...])


def swiglu_ln(x, mod, w_gu, w_down, ln_g, ln_b, seq_len, *, tm=512, tf=1408):
    t_rows = x.shape[0]
    nf = D_FF // tf
    row = pl.BlockSpec((tm, D_MODEL), lambda i, f: (i, 0))
    vec = pl.BlockSpec((1, D_MODEL), lambda i, f: (0, 0))
    return pl.pallas_call(
        _ffn_kernel,
        grid=(t_rows // tm, nf),
        in_specs=[row, _mod_spec(4, tm, seq_len), _mod_spec(3, tm, seq_len), _mod_spec(5, tm, seq_len),
                  pl.BlockSpec((D_MODEL, tf), lambda i, f: (0, f)),
                  pl.BlockSpec((D_MODEL, tf), lambda i, f: (0, nf + f)),
                  pl.BlockSpec((tf, D_MODEL), lambda i, f: (f, 0)),
                  vec, vec],
        out_specs=row,
        out_shape=jax.ShapeDtypeStruct((t_rows, D_MODEL), F32),
        scratch_shapes=[pltpu.VMEM((tm, D_MODEL), BF16), pltpu.VMEM((tm, D_MODEL), F32)],
        compiler_params=_params("parallel", "arbitrary"),
        name="swiglu_ln",
    )(x, mod, mod, mod, w_gu, w_gu, w_down, ln_g.reshape(1, D_MODEL), ln_b.reshape(1, D_MODEL))


MOE_SUB = 128


def _moe_kernel(x_ref, sc_ref, sh_ref, gt_ref, r_ref, wg_ref, wu_ref, wd_ref, g_ref, b_ref, o_ref,
                hb_ref, comb_ref, slot_ref, slot_t_ref, cnt_ref, xg_ref, yg_ref, acc_ref):
    e = pl.program_id(1)
    f = pl.program_id(2)
    tm = x_ref.shape[0]
    sub = MOE_SUB

    @pl.when((e == 0) & (f == 0))
    def _():
        h = x_ref[...] * (1.0 + sc_ref[0]) + sh_ref[0]
        hb_ref[...] = h.astype(BF16)
        acc_ref[...] = jnp.zeros_like(acc_ref)
        logits = _dot_split(h, r_ref[...])
        lane = lax.broadcasted_iota(jnp.int32, logits.shape, 1)
        neg = jnp.float32(-jnp.inf)
        lg = jnp.where(lane < N_EXPERTS, logits, neg)
        m1 = jnp.max(lg, -1, keepdims=True)
        i1 = jnp.min(jnp.where(lg == m1, lane, LANES), -1, keepdims=True)
        lg2 = jnp.where(lane == i1, neg, lg)
        m2 = jnp.max(lg2, -1, keepdims=True)
        i2 = jnp.min(jnp.where(lg2 == m2, lane, LANES), -1, keepdims=True)
        e2 = jnp.exp(m2 - m1)
        g1 = 1.0 / (1.0 + e2)
        comb = jnp.where(lane == i1, g1, 0.0) + jnp.where(lane == i2, e2 * g1, 0.0)
        comb_ref[...] = comb
        sel = comb > 0.0
        selb = jnp.where(sel, 1.0, 0.0).astype(BF16)
        chunk = 256
        for r0 in range(0, tm, chunk):
            rr = lax.broadcasted_iota(jnp.int32, (chunk, tm), 0) + r0
            cc = lax.broadcasted_iota(jnp.int32, (chunk, tm), 1)
            before = jnp.where(cc < rr, 1.0, 0.0).astype(BF16)
            slot_ref[r0:r0 + chunk, :] = _dot(before, selb)
        slot = jnp.where(sel, slot_ref[...], -1.0)
        slot_ref[...] = slot
        slot_t_ref[...] = slot.T
        cnt_ref[...] = jnp.sum(jnp.where(sel, 1.0, 0.0), 0, keepdims=True)

    lane1 = lax.broadcasted_iota(jnp.int32, cnt_ref.shape, 1)
    n_e = jnp.sum(jnp.where(lane1 == e, cnt_ref[...], 0.0)).astype(jnp.int32)
    n_sub = (n_e + sub - 1) // sub

    @pl.when(f == 0)
    def _():
        slot_row = slot_t_ref[pl.ds(e, 1), :]

        def gather(s, carry):
            base = pl.multiple_of(s * sub, sub)
            want = (lax.broadcasted_iota(jnp.int32, (sub, tm), 0) + base).astype(F32)
            pick = jnp.where(slot_row == want, 1.0, 0.0).astype(BF16)
            xg_ref[pl.ds(base, sub), :] = _dot(pick, hb_ref[...]).astype(BF16)
            yg_ref[pl.ds(base, sub), :] = jnp.zeros((sub, D_MODEL), F32)
            return carry

        lax.fori_loop(0, n_sub, gather, 0)

    def expert_ffn(s, carry):
        base = pl.multiple_of(s * sub, sub)
        xs = xg_ref[pl.ds(base, sub), :]
        gate = _dot(xs, wg_ref[...])
        up = _dot(xs, wu_ref[...])
        act = (gate * _sigmoid(gate) * up).astype(BF16)
        yg_ref[pl.ds(base, sub), :] += _dot(act, wd_ref[...])
        return carry

    lax.fori_loop(0, n_sub, expert_ffn, 0)

    @pl.when(f == pl.num_programs(2) - 1)
    def _():
        lane = lax.broadcasted_iota(jnp.int32, comb_ref.shape, 1)
        mine = lane == e
        slot_col = jnp.sum(jnp.where(mine, slot_ref[...], 0.0), -1, keepdims=True)
        c_e = jnp.sum(jnp.where(mine, comb_ref[...], 0.0), -1, keepdims=True)

        def scatter(s, carry):
            base = pl.multiple_of(s * sub, sub)
            have = (lax.broadcasted_iota(jnp.int32, (tm, sub), 1) + base).astype(F32)
            put = jnp.where(slot_col == have, 1.0, 0.0).astype(BF16)
            acc_ref[...] += c_e * _dot(put, yg_ref[pl.ds(base, sub), :].astype(BF16))
            return carry

        lax.fori_loop(0, n_sub, scatter, 0)

    @pl.when((e == pl.num_programs(1) - 1) & (f == pl.num_programs(2) - 1))
    def _():
        o_ref[...] = _layer_norm(DEEPNORM_ALPHA * x_ref[...] + (1.0 + gt_ref[0]) * acc_ref[...],
                                 g_ref[...], b_ref[...])


def moe_ln(x, mod, router, w_gu, w_down, ln_g, ln_b, seq_len, *, tm=1024, tf=512):
    t_rows = x.shape[0]
    nf = D_FF_EXPERT // tf
    router_p = jnp.pad(router, ((0, 0), (0, LANES - N_EXPERTS)))
    row = pl.BlockSpec((tm, D_MODEL), lambda i, e, f: (i, 0))
    vec = pl.BlockSpec((1, D_MODEL), lambda i, e, f: (0, 0))
    return pl.pallas_call(
        _moe_kernel,
        grid=(t_rows // tm, N_EXPERTS, nf),
        in_specs=[row, _mod_spec(4, tm, seq_len), _mod_spec(3, tm, seq_len), _mod_spec(5, tm, seq_len),
                  pl.BlockSpec((D_MODEL, LANES), lambda i, e, f: (0, 0)),
                  pl.BlockSpec((None, D_MODEL, tf), lambda i, e, f: (e, 0, f)),
                  pl.BlockSpec((None, D_MODEL, tf), lambda i, e, f: (e, 0, nf + f)),
                  pl.BlockSpec((None, tf, D_MODEL), lambda i, e, f: (e, f, 0)),
                  vec, vec],
        out_specs=row,
        out_shape=jax.ShapeDtypeStruct((t_rows, D_MODEL), F32),
        scratch_shapes=[pltpu.VMEM((tm, D_MODEL), BF16),
                        pltpu.VMEM((tm, LANES), F32),
                        pltpu.VMEM((tm, LANES), F32),
                        pltpu.VMEM((LANES, tm), F32),
                        pltpu.VMEM((1, LANES), F32),
                        pltpu.VMEM((tm, D_MODEL), BF16),
                        pltpu.VMEM((tm, D_MODEL), F32),
                        pltpu.VMEM((tm, D_MODEL), F32)],
        compiler_params=_params("parallel", "arbitrary", "arbitrary"),
        name="moe_ln",
    )(x, mod, mod, mod, router_p, w_gu, w_gu, w_down, ln_g.reshape(1, D_MODEL), ln_b.reshape(1, D_MODEL))


def _short_conv_kernel(u_ref, w_ref, b_ref, o_ref):
    u = u_ref[...]
    n = u.shape[0]
    row = lax.broadcasted_iota(jnp.int32, u.shape, 0)
    prev = jnp.where(row == 0, 0.0, pltpu.roll(u, 1, axis=0))
    nxt = jnp.where(row == n - 1, 0.0, pltpu.roll(u, n - 1, axis=0))
    o_ref[...] = prev * w_ref[0:1, :] + u * w_ref[1:2, :] + nxt * w_ref[2:3, :] + b_ref[...]


def short_conv(u, conv_w, conv_b, *, tn=256):
    nb, seq_len, width = u.shape
    blk = pl.BlockSpec((None, seq_len, tn), lambda b, j: (b, 0, j))
    return pl.pallas_call(
        _short_conv_kernel,
        grid=(nb, width // tn),
        in_specs=[blk, pl.BlockSpec((3, tn), lambda b, j: (0, j)), pl.BlockSpec((1, tn), lambda b, j: (0, j))],
        out_specs=blk,
        out_shape=jax.ShapeDtypeStruct(u.shape, F32),
        compiler_params=_params("parallel", "parallel"),
        name="short_conv",
    )(u, conv_w, conv_b.reshape(1, width))


def _filter_kernel(z_ref, w1_ref, b1_ref, w2_ref, b2_ref, w3_ref, b3_ref, w4_ref, fr_ref, dl_ref,
                   sum_ref, dif_ref, nrm_ref, nyq_ref):
    i = pl.program_id(0)
    z = z_ref[...]
    fr = fr_ref[...]
    h = jnp.sin(fr * (_dot_split(z, w1_ref[...]) + b1_ref[...]))
    h = jnp.sin(fr * (_dot_split(h, w2_ref[...]) + b2_ref[...]))
    h = jnp.sin(fr * (_dot_split(h, w3_ref[...]) + b3_ref[...]))
    hf = _dot_split(h, w4_ref[...])
    win = jnp.exp(-z[:, 0:1] * dl_ref[...])
    win = jnp.concatenate([win] * HYENA_ORDER, axis=1)
    half = HYENA_ORDER * D_MODEL
    row = lax.broadcasted_iota(jnp.int32, (z.shape[0], half), 0) + i * z.shape[0]
    kf = hf[:, :half] * win
    kb = jnp.where(row == 0, 0.0, hf[:, half:] * win)
    ksum = kf + kb
    sum_ref[...] = ksum.astype(sum_ref.dtype)
    dif_ref[...] = (kf - kb).astype(dif_ref.dtype)
    sign = (1 - 2 * (row % 2)).astype(F32)

    @pl.when(i == 0)
    def _():
        nrm_ref[...] = jnp.zeros_like(nrm_ref)
        nyq_ref[...] = jnp.zeros_like(nyq_ref)

    nrm_ref[...] += jnp.sum(jnp.abs(kf) + jnp.abs(kb), 0, keepdims=True)
    nyq_ref[...] += jnp.sum(ksum * sign, 0, keepdims=True)


def hyena_filter_taps(feat, w1p, b1, w2, b2, w3, b3, w4, freq, deltas, *, tl=256):
    seq_len = feat.shape[0]
    half = HYENA_ORDER * D_MODEL
    full = lambda a: pl.BlockSpec(a.shape, lambda i: (0,) * a.ndim)
    args = [feat, w1p, b1.reshape(1, -1), w2, b2.reshape(1, -1), w3, b3.reshape(1, -1), w4,
            freq.reshape(1, -1), deltas.reshape(1, -1)]
    return pl.pallas_call(
        _filter_kernel,
        grid=(seq_len // tl,),
        in_specs=[pl.BlockSpec((tl, LANES), lambda i: (i, 0))] + [full(a) for a in args[1:]],
        out_specs=[pl.BlockSpec((tl, half), lambda i: (i, 0)), pl.BlockSpec((tl, half), lambda i: (i, 0)),
                   pl.BlockSpec((1, half), lambda i: (0, 0)), pl.BlockSpec((1, half), lambda i: (0, 0))],
        out_shape=[jax.ShapeDtypeStruct((seq_len, half), BF16), jax.ShapeDtypeStruct((seq_len, half), BF16),
                   jax.ShapeDtypeStruct((1, half), F32), jax.ShapeDtypeStruct((1, half), F32)],
        compiler_params=_params("arbitrary"),
        name="hyena_filter_taps",
    )(*args)


def _spectrum_kernel(m_ref, k_ref, nrm_ref, nyq_ref, o_ref, acc_ref, *, patch_row0):
    kk = pl.program_id(2)

    @pl.when(kk == 0)
    def _():
        acc_ref[...] = jnp.zeros_like(acc_ref)

    acc_ref[...] += _dot(m_ref[...], k_ref[...])

    @pl.when(kk == pl.num_programs(2) - 1)
    def _():
        inv = 1.0 / (nrm_ref[...] + FILTER_NORM_EPS)
        out = acc_ref[...] * inv
        if patch_row0:
            row = lax.broadcasted_iota(jnp.int32, out.shape, 0) + pl.program_id(0) * out.shape[0]
            out = jnp.where(row == 0, nyq_ref[...] * inv, out)
        o_ref[...] = out


def filter_spectrum(mat, taps, nrm, nyq, *, patch_row0, tm=512, tn=512, tk=512):
    seq_len, width = taps.shape
    return pl.pallas_call(
        functools.partial(_spectrum_kernel, patch_row0=patch_row0),
        grid=(seq_len // tm, width // tn, seq_len // tk),
        in_specs=[pl.BlockSpec((tm, tk), lambda i, j, k: (i, k)),
                  pl.BlockSpec((tk, tn), lambda i, j, k: (k, j)),
                  pl.BlockSpec((1, tn), lambda i, j, k: (0, j)),
                  pl.BlockSpec((1, tn), lambda i, j, k: (0, j))],
        out_specs=pl.BlockSpec((tm, tn), lambda i, j, k: (i, j)),
        out_shape=jax.ShapeDtypeStruct((seq_len, width), F32),
        scratch_shapes=[pltpu.VMEM((tm, tn), F32)],
        compiler_params=_params("parallel", "parallel", "arbitrary"),
        name="filter_spectrum",
    )(mat, taps, nrm, nyq)


def _long_conv_kernel(z_ref, gate_ref, hr_ref, hi_ref, skip_ref, fc_ref, fs_ref, gc_ref, gs_ref, o_ref,
                      zb_ref, acc_ref):
    f = pl.program_id(2)

    @pl.when(f == 0)
    def _():
        zb_ref[...] = z_ref[...].astype(BF16)
        acc_ref[...] = jnp.zeros_like(acc_ref)

    zb = zb_ref[...]
    xr = _dot(fc_ref[...], zb)
    xi = _dot(fs_ref[...], zb)
    hr = hr_ref[...]
    hi = hi_ref[...]
    row = lax.broadcasted_iota(jnp.int32, xr.shape, 0)
    packed = (row == 0) & (f == 0)
    pr = jnp.where(packed, xr * hr, xr * hr - xi * hi)
    pi = jnp.where(packed, xi * hi, xr * hi + xi * hr)
    acc_ref[...] += _dot(gc_ref[...], pr.astype(BF16)) + _dot(gs_ref[...], pi.astype(BF16))

    @pl.when(f == pl.num_programs(2) - 1)
    def _():
        conv = acc_ref[...] * (1.0 / z_ref.shape[0])
        o_ref[...] = gate_ref[...] * (conv + z_ref[...] * skip_ref[...])


def long_conv_gate(zsrc, z_col0, gsrc, g_col0, spec_r, spec_i, order, skip, mats, *, tn, tf):
    nb, seq_len, _ = zsrc.shape
    fc, fs, gc, gs = mats
    nct = D_MODEL // tn
    return pl.pallas_call(
        _long_conv_kernel,
        grid=(nb, nct, seq_len // tf),
        in_specs=[pl.BlockSpec((None, seq_len, tn), lambda b, c, f: (b, 0, z_col0 + c)),
                  pl.BlockSpec((None, seq_len, tn), lambda b, c, f: (b, 0, g_col0 + c)),
                  pl.BlockSpec((tf, tn), lambda b, c, f: (f, order * nct + c)),
                  pl.BlockSpec((tf, tn), lambda b, c, f: (f, order * nct + c)),
                  pl.BlockSpec((None, 1, tn), lambda b, c, f: (order, 0, c)),
                  pl.BlockSpec((tf, seq_len), lambda b, c, f: (f, 0)),
                  pl.BlockSpec((tf, seq_len), lambda b, c, f: (f, 0)),
                  pl.BlockSpec((seq_len, tf), lambda b, c, f: (0, f)),
                  pl.BlockSpec((seq_len, tf), lambda b, c, f: (0, f))],
        out_specs=pl.BlockSpec((None, seq_len, tn), lambda b, c, f: (b, 0, c)),
        out_shape=jax.ShapeDtypeStruct((nb, seq_len, D_MODEL), F32),
        scratch_shapes=[pltpu.VMEM((seq_len, tn), BF16), pltpu.VMEM((seq_len, tn), F32)],
        compiler_params=_params("parallel", "parallel", "arbitrary"),
        name="long_conv_gate",
    )(zsrc, gsrc, spec_r, spec_i, skip.reshape(HYENA_ORDER, 1, D_MODEL), fc, fs, gc, gs)


def dft_matrices(seq_len):
    idx = jnp.arange(seq_len, dtype=jnp.int32)
    m = (idx[:, None] * idx[None, :]) % (2 * seq_len)
    ang = m.astype(F32) * (math.pi / seq_len)
    cosm = jnp.cos(ang)
    sinm = -jnp.sin(ang)
    alt = (1 - 2 * (idx % 2)).astype(F32)
    fc = cosm
    fs = sinm.at[0, :].set(alt)
    gc = cosm.at[:, 0].set(0.5)
    gs = sinm.at[:, 0].set(0.5 * alt)
    return tuple(a.astype(BF16) for a in (fc, fs, gc, gs))


def filter_features(seq_len):
    t = jnp.linspace(0.0, 1.0, seq_len, dtype=F32)[:, None]
    omega = 2.0 * math.pi * jnp.arange(seq_len, dtype=F32)[:, None] / seq_len
    bands = jnp.linspace(1e-4, FILTER_BANDS - 1, FILTER_BANDS, dtype=F32)[None, :]
    z = jnp.concatenate([t, jnp.cos(bands * omega), -jnp.sin(bands * omega)], -1)
    return jnp.pad(z, ((0, 0), (0, LANES - z.shape[1])))


def hyena_mixer(x, mod, hp, nb, seq_len):
    tn = 512 if seq_len <= 2048 else 256
    u = modulated_projection(x, mod, hp["w_in"], seq_len)
    u = short_conv(u.reshape(nb, seq_len, 3 * D_MODEL), hp["conv_w"], hp["conv_b"])
    mats = dft_matrices(seq_len)
    deltas = jnp.abs(jnp.linspace(HYENA_MIN_DECAY, HYENA_MAX_DECAY, D_MODEL, dtype=F32))
    ksum, kdif, nrm, nyq = hyena_filter_taps(filter_features(seq_len), hp["f_w1p"], hp["f_b1"], hp["f_w2"],
                                             hp["f_b2"], hp["f_w3"], hp["f_b3"], hp["f_w4"], hp["f_freq"], deltas)
    spec_r = filter_spectrum(mats[0], ksum, nrm, nyq, patch_row0=False)
    spec_i = filter_spectrum(mats[1], kdif, nrm, nyq, patch_row0=True)
    nct = D_MODEL // tn
    z = long_conv_gate(u, 0, u, nct, spec_r, spec_i, 0, hp["skip"], mats, tn=tn, tf=256)
    z = long_conv_gate(z, 0, u, 2 * nct, spec_r, spec_i, 1, hp["skip"], mats, tn=tn, tf=256)
    return z.reshape(nb * seq_len, D_MODEL)


def _rwkv_prep_kernel(x_ref, xp_ref, xn_ref, sc_ref, sh_ref, mu_ref, wrkv_ref, w1_ref, w2_ref, a1_ref, a2_ref,
                      g1_ref, g2_ref, w0_ref, a0_ref,
                      r_ref, k_ref, v_ref, dec0_ref, dec1_ref, as0_ref, as1_ref, g_ref, *, seq_len):
    i = pl.program_id(0)
    tm = x_ref.shape[0]
    sc = 1.0 + sc_ref[0]
    sh = sh_ref[0]
    h = x_ref[...] * sc + sh
    row = lax.broadcasted_iota(jnp.int32, h.shape, 0)
    pos = (row + i * tm) % seq_len
    halo_prev = xp_ref[7:8, :] * sc + sh
    halo_next = xn_ref[0:1, :] * sc + sh
    prev = jnp.where(row == 0, halo_prev, pltpu.roll(h, 1, axis=0))
    prev = jnp.where(pos == 0, 0.0, prev)
    nxt = jnp.where(row == tm - 1, halo_next, pltpu.roll(h, tm - 1, axis=0))
    nxt = jnp.where(pos == seq_len - 1, 0.0, nxt)
    xx = 0.5 * (prev + nxt) - h

    def mix(j):
        return (h + xx * mu_ref[j:j + 1, :]).astype(BF16)

    r_ref[...] = _dot(mix(0), wrkv_ref[0])
    k_ref[...] = _dot(mix(1), wrkv_ref[1])
    v_ref[...] = _dot(mix(2), wrkv_ref[2])
    lw = _dot(jnp.tanh(_dot(mix(3), w1_ref[...])).astype(BF16), w2_ref[...])
    la = _dot(_dot(mix(4), a1_ref[...]).astype(BF16), a2_ref[...])
    g_ref[...] = _dot(_sigmoid(_dot(mix(5), g1_ref[...])).astype(BF16), g2_ref[...])
    for d, (dec_ref, as_ref) in enumerate(((dec0_ref, as0_ref), (dec1_ref, as1_ref))):
        cols = slice(d * D_MODEL, (d + 1) * D_MODEL)
        pre = -(w0_ref[d:d + 1, :] + lw[:, cols])
        softplus = jnp.maximum(pre, 0.0) + jnp.log(1.0 + jnp.exp(-jnp.abs(pre)))
        dec_ref[...] = jnp.exp(-jnp.exp(-softplus - 0.5))
        as_ref[...] = _sigmoid(a0_ref[d:d + 1, :] + la[:, cols])


def rwkv_prep(x, mod, rp, seq_len, *, tm=256):
    t_rows = x.shape[0]
    row = pl.BlockSpec((tm, D_MODEL), lambda i: (i, 0))
    halo = tm // 8
    last8 = t_rows // 8 - 1
    full = lambda a: pl.BlockSpec(a.shape, lambda i: (0,) * a.ndim)
    weights = [rp["mu"], rp["w_rkv"], rp["w1"], rp["w2"], rp["a1"], rp["a2"], rp["g1"], rp["g2"], rp["w0"], rp["a0"]]
    return pl.pallas_call(
        functools.partial(_rwkv_prep_kernel, seq_len=seq_len),
        grid=(t_rows // tm,),
        in_specs=[row,
                  pl.BlockSpec((8, D_MODEL), lambda i: (jnp.maximum(i * halo - 1, 0), 0)),
                  pl.BlockSpec((8, D_MODEL), lambda i: (jnp.minimum((i + 1) * halo, last8), 0)),
                  _mod_spec(1, tm, seq_len), _mod_spec(0, tm, seq_len)] + [full(a) for a in weights],
        out_specs=[row] * 8,
        out_shape=[jax.ShapeDtypeStruct((t_rows, D_MODEL), F32)] * 8,
        compiler_params=_params("parallel"),
        name="rwkv_prep",
    )(x, x, x, mod, mod, *weights)


def _wkv_scan_kernel(*refs, reverse, tt):
    if reverse:
        (r_ref, k_ref, v_ref, w_ref, as_ref, kkp_ref, kap_ref, rkp_ref, gng_ref, gnb_ref, yin_ref, bin_ref,
         y_ref, s_ref, na_ref, b_ref, kd_ref) = refs
    else:
        (r_ref, k_ref, v_ref, w_ref, as_ref, kkp_ref, kap_ref, rkp_ref,
         y_ref, bout_ref, s_ref, na_ref, b_ref, kd_ref) = refs
    n = RWKV_HEAD

    @pl.when(pl.program_id(1) == 0)
    def _():
        s_ref[...] = jnp.zeros_like(s_ref)

    r = r_ref[...]
    k = k_ref[...]
    a_sig = as_ref[...]
    kk = k * kkp_ref[...][None]
    kk = kk / jnp.maximum(jnp.sqrt(jnp.sum(kk * kk, 1, keepdims=True)), 1e-12)
    kd = k * (1.0 + (a_sig - 1.0) * kap_ref[...][None])
    na_ref[...] = -kk
    b_ref[...] = kk * a_sig
    kd_ref[...] = kd
    bonus = jnp.sum(r * kd * rkp_ref[...][None], 1, keepdims=True)

    def time_index(j):
        return tt - 1 - j if reverse else j

    a_first = na_ref[time_index(0)]
    sa0 = jnp.zeros((n, LANES), F32)
    for kc in range(n):
        sa0 = sa0 + s_ref[kc] * a_first[kc:kc + 1, :]

    def step(j, sa):
        t = time_index(j)
        a_next = na_ref[time_index(jnp.minimum(j + 1, tt - 1))]
        w_t = w_ref[t]
        b_t = b_ref[t]
        kd_t = kd_ref[t]
        r_t = r_ref[t]
        v_t = v_ref[t]
        y = jnp.zeros((n, LANES), F32)
        sa_next = jnp.zeros((n, LANES), F32)
        for kc in range(n):
            s_new = s_ref[kc] * w_t[kc:kc + 1, :] + sa * b_t[kc:kc + 1, :] + v_t * kd_t[kc:kc + 1, :]
            s_ref[kc] = s_new
            y = y + s_new * r_t[kc:kc + 1, :]
            sa_next = sa_next + s_new * a_next[kc:kc + 1, :]
        y_ref[t] = y
        return sa_next

    lax.fori_loop(0, tt, step, sa0)

    if reverse:
        y = y_ref[...] + yin_ref[...]
        mean = jnp.mean(y, 1, keepdims=True)
        yc = y - mean
        var = jnp.mean(yc * yc, 1, keepdims=True)
        yn = yc * lax.rsqrt(var + RWKV_GN_EPS) * gng_ref[...][None] + gnb_ref[...][None]
        y_ref[...] = yn + (bonus + bin_ref[...]) * v_ref[...]
    else:
        bout_ref[...] = bonus


def wkv_scan(r, k, v, w, a_sig, chan, *, reverse, y_in=None, bonus_in=None, tt=32):
    seq_len, n, chains = r.shape
    nt = seq_len // tt
    tmap = (lambda c, j: (nt - 1 - j, 0, c)) if reverse else (lambda c, j: (j, 0, c))
    blk = pl.BlockSpec((tt, n, LANES), tmap)
    bblk = pl.BlockSpec((tt, 1, LANES), tmap)
    par = pl.BlockSpec((n, LANES), lambda c, j: (0, 0))
    in_specs = [blk] * 5 + [par] * 3
    args = [r, k, v, w, a_sig, chan["k_k"], chan["k_a"], chan["r_k"]]
    if reverse:
        in_specs += [par, par, blk, bblk]
        args += [chan["gn_g"], chan["gn_b"], y_in, bonus_in]
        out_specs = blk
        out_shape = jax.ShapeDtypeStruct(r.shape, F32)
    else:
        out_specs = [blk, bblk]
        out_shape = [jax.ShapeDtypeStruct(r.shape, F32), jax.ShapeDtypeStruct((seq_len, 1, chains), F32)]
    return pl.pallas_call(
        functools.partial(_wkv_scan_kernel, reverse=reverse, tt=tt),
        grid=(chains // LANES, nt),
        in_specs=in_specs,
        out_specs=out_specs,
        out_shape=out_shape,
        scratch_shapes=[pltpu.VMEM((n, n, LANES), F32)] + [pltpu.VMEM((tt, n, LANES), F32)] * 3,
        compiler_params=_params("parallel", "arbitrary"),
        name="wkv_scan_rev" if reverse else "wkv_scan_fwd",
    )(*args)


def rwkv_mixer(x, mod, rp, nb, seq_len):
    r, k, v, dec0, dec1, as0, as1, g = rwkv_prep(x, mod, rp, seq_len)
    chains = nb * RWKV_HEADS

    def to_chains(a):
        return a.reshape(nb, seq_len, RWKV_HEADS, RWKV_HEAD).transpose(1, 3, 0, 2).reshape(seq_len, RWKV_HEAD, chains)

    rt, kt, vt = to_chains(r), to_chains(k), to_chains(v)
    y_f, bonus_f = wkv_scan(rt, kt, vt, to_chains(dec0), to_chains(as0), rp["chan"][0], reverse=False)
    y = wkv_scan(rt, kt, vt, to_chains(dec1), to_chains(as1), rp["chan"][1], reverse=True,
                 y_in=y_f, bonus_in=bonus_f)
    y = y.reshape(seq_len, RWKV_HEAD, nb, RWKV_HEADS).transpose(2, 0, 3, 1).reshape(nb * seq_len, D_MODEL)
    return y, g


def _diff_attn_kernel(lam_ref, g_ref, q_ref, k_ref, v_ref, o_ref, *, lam_init):
    lf = lam_ref[...]
    lam = (jnp.exp(jnp.sum(lf[0:1] * lf[1:2], -1, keepdims=True))
           - jnp.exp(jnp.sum(lf[2:3] * lf[3:4], -1, keepdims=True)) + lam_init)
    q = q_ref[...]
    k = k_ref[...]

    def softmax_terms(cols):
        s = lax.dot_general(q[:, cols], k[:, cols], (((1,), (1,)), ((), ())), preferred_element_type=F32)
        e = jnp.exp(s - jnp.max(s, -1, keepdims=True))
        return e, 1.0 / jnp.sum(e, -1, keepdims=True)

    e1, inv1 = softmax_terms(slice(0, DA_HEAD))
    e2, inv2 = softmax_terms(slice(DA_HEAD, 2 * DA_HEAD))
    a = e1 * inv1 - e2 * (lam * inv2)
    o = _dot(a.astype(BF16), v_ref[...])
    o = o * lax.rsqrt(jnp.mean(o * o, -1, keepdims=True) + SUBLN_EPS) * g_ref[...] * (1.0 - lam_init)
    o_ref[...] = o.astype(o_ref.dtype)


def diff_attention(qkv, lam, subln_g, layer_idx, nb, seq_len, *, tq=256):
    lam_init = 0.8 - 0.6 * math.exp(-0.3 * layer_idx)
    hd = 2 * DA_HEAD
    nq = seq_len // tq
    return pl.pallas_call(
        functools.partial(_diff_attn_kernel, lam_init=lam_init),
        grid=(nb, DA_HEADS, nq),
        in_specs=[pl.BlockSpec((4, DA_HEAD), lambda b, h, i: (0, 0)),
                  pl.BlockSpec((1, hd), lambda b, h, i: (0, 0)),
                  pl.BlockSpec((tq, hd), lambda b, h, i: (b * nq + i, h)),
                  pl.BlockSpec((seq_len, hd), lambda b, h, i: (b, DA_HEADS + h)),
                  pl.BlockSpec((seq_len, hd), lambda b, h, i: (b, 2 * DA_HEADS + h))],
        out_specs=pl.BlockSpec((tq, hd), lambda b, h, i: (b * nq + i, h)),
        out_shape=jax.ShapeDtypeStruct((nb * seq_len, D_MODEL), BF16),
        compiler_params=_params("parallel", "parallel", "arbitrary"),
        name="diff_attention",
    )(lam, subln_g.reshape(1, hd), qkv, qkv, qkv)


def _chan_table(p):
    t = p.reshape(RWKV_HEADS, RWKV_HEAD).T
    return jnp.tile(t, (1, LANES // RWKV_HEADS))


def _prepare_weights(P):
    bf = lambda a: a.astype(BF16)
    W = {"hy": [], "rw": [], "da": []}
    for j in range(P["hy_w_in"].shape[0]):
        W["hy"].append(dict(
            w_in=bf(P["hy_w_in"][j]), conv_w=P["hy_conv_w"][j], conv_b=P["hy_conv_b"][j],
            f_w1p=jnp.pad(P["hy_f_w1"][j], ((0, LANES - P["hy_f_w1"].shape[1]), (0, 0))),
            f_b1=P["hy_f_b1"][j], f_w2=P["hy_f_w2"][j], f_b2=P["hy_f_b2"][j], f_w3=P["hy_f_w3"][j],
            f_b3=P["hy_f_b3"][j], f_w4=P["hy_f_w4"][j], f_freq=P["hy_f_freq"][j], skip=P["hy_bias"][j],
            w_out=bf(P["hy_w_out"][j])))
    for j in range(P["rw_w_rkv"].shape[0]):
        zero = jnp.zeros((RWKV_LORA, D_MODEL), F32)
        blockdiag = lambda m: jnp.concatenate([jnp.concatenate([m[0], zero], 1), jnp.concatenate([zero, m[1]], 1)], 0)
        gpad = 2 * LANES - GATE_LORA
        W["rw"].append(dict(
            mu=P["rw_mu"][j], w_rkv=bf(P["rw_w_rkv"][j]),
            w1=bf(jnp.concatenate([P["rw_w1"][j, 0], P["rw_w1"][j, 1]], 1)), w2=bf(blockdiag(P["rw_w2"][j])),
            a1=bf(jnp.concatenate([P["rw_a1"][j, 0], P["rw_a1"][j, 1]], 1)), a2=bf(blockdiag(P["rw_a2"][j])),
            g1=bf(jnp.pad(P["rw_g1"][j], ((0, 0), (0, gpad)))), g2=bf(jnp.pad(P["rw_g2"][j], ((0, gpad), (0, 0)))),
            w0=P["rw_w0"][j], a0=P["rw_a0"][j],
            chan=[dict(k_k=_chan_table(P["rw_k_k"][j]), k_a=_chan_table(P["rw_k_a"][j]),
                       r_k=_chan_table(P["rw_r_k"][j, d]), gn_g=_chan_table(P["rw_gn_g"][j]),
                       gn_b=_chan_table(P["rw_gn_b"][j])) for d in range(2)],
            w_out=bf(P["rw_w_out"][j])))
    for j in range(P["da_w_qkv"].shape[0]):
        W["da"].append(dict(w_qkv=bf(P["da_w_qkv"][j]), lam=P["da_lam"][j], subln_g=P["da_subln_g"][j],
                            w_out=bf(P["da_w_out"][j])))
    W["ff_w_gu"], W["ff_w_down"] = bf(P["ff_w_gu"]), bf(P["ff_w_down"])
    W["moe_w_gu"], W["moe_w_down"] = bf(P["moe_w_gu"]), bf(P["moe_w_down"])
    return W


def _trunk(x, mods, P, W):
    nb, seq_len, _ = x.shape
    x = x.reshape(nb * seq_len, D_MODEL)
    for i in range(DEPTH):
        mod = mods[i].reshape(nb * 6, 1, D_MODEL)
        kind, j = i % N_MIXERS, i // N_MIXERS
        zgate = None
        if kind == 0:
            z = hyena_mixer(x, mod, W["hy"][j], nb, seq_len)
            w_out = W["hy"][j]["w_out"]
        elif kind == 1:
            z, zgate = rwkv_mixer(x, mod, W["rw"][j], nb, seq_len)
            w_out = W["rw"][j]["w_out"]
        else:
            da = W["da"][j]
            qkv = qkv_rope_projection(x, mod, da["w_qkv"], seq_len)
            z = diff_attention(qkv, da["lam"], da["subln_g"], i, nb, seq_len)
            w_out = da["w_out"]
        x = out_projection_ln(z, x, mod, w_out, P["ln_g"][i, 0], P["ln_b"][i, 0], seq_len, zgate=zgate)
        if i % 2 == 0:
            x = swiglu_ln(x, mod, W["ff_w_gu"][i // 2], W["ff_w_down"][i // 2],
                          P["ln_g"][i, 1], P["ln_b"][i, 1], seq_len)
        else:
            x = moe_ln(x, mod, P["moe_router"][i // 2], W["moe_w_gu"][i // 2], W["moe_w_down"][i // 2],
                       P["ln_g"][i, 1], P["ln_b"][i, 1], seq_len)
    return x.reshape(nb, seq_len, D_MODEL)


def kernel(x_prompt, x_sample, c_prompt, c_sample, ada_w, ada_b, ln_g, ln_b, hy_w_in, hy_conv_w, hy_conv_b, hy_f_w1, hy_f_b1, hy_f_w2, hy_f_b2, hy_f_w3, hy_f_b3, hy_f_w4, hy_f_freq, hy_bias, hy_w_out, rw_mu, rw_w_rkv, rw_w0, rw_w1, rw_w2, rw_a0, rw_a1, rw_a2, rw_g1, rw_g2, rw_k_k, rw_k_a, rw_r_k, rw_gn_g, rw_gn_b, rw_w_out, da_w_qkv, da_lam, da_subln_g, da_w_out, ff_w_gu, ff_w_down, moe_router, moe_w_gu, moe_w_down):
    P = dict(ada_w=ada_w, ada_b=ada_b, ln_g=ln_g, ln_b=ln_b,
             hy_w_in=hy_w_in, hy_conv_w=hy_conv_w, hy_conv_b=hy_conv_b,
             hy_f_w1=hy_f_w1, hy_f_b1=hy_f_b1, hy_f_w2=hy_f_w2, hy_f_b2=hy_f_b2,
             hy_f_w3=hy_f_w3, hy_f_b3=hy_f_b3, hy_f_w4=hy_f_w4, hy_f_freq=hy_f_freq,
             hy_bias=hy_bias, hy_w_out=hy_w_out,
             rw_mu=rw_mu, rw_w_rkv=rw_w_rkv, rw_w0=rw_w0, rw_w1=rw_w1, rw_w2=rw_w2,
             rw_a0=rw_a0, rw_a1=rw_a1, rw_a2=rw_a2, rw_g1=rw_g1, rw_g2=rw_g2,
             rw_k_k=rw_k_k, rw_k_a=rw_k_a, rw_r_k=rw_r_k, rw_gn_g=rw_gn_g, rw_gn_b=rw_gn_b,
             rw_w_out=rw_w_out,
             da_w_qkv=da_w_qkv, da_lam=da_lam, da_subln_g=da_subln_g, da_w_out=da_w_out,
             ff_w_gu=ff_w_gu, ff_w_down=ff_w_down,
             moe_router=moe_router, moe_w_gu=moe_w_gu, moe_w_down=moe_w_down)
    W = _prepare_weights(P)
    n_prompt = x_prompt.shape[0]
    mods = ada_modulation(jnp.concatenate([c_prompt, c_sample], 0), ada_w, ada_b)
    y_prompt = _trunk(x_prompt, mods[:, :n_prompt], P, W)
    y_sample = _trunk(x_sample, mods[:, n_prompt:], P, W)
    return (y_prompt, y_sample)
```

```python
import functools
import math

import jax
import jax.numpy as jnp
from jax import lax
from jax.experimental import pallas as pl
from jax.experimental.pallas import tpu as pltpu

F32 = jnp.float32
BF16 = jnp.bfloat16

D_MODEL = 1024
DEPTH = 4
N_MIXERS = 3
DEEPNORM_ALPHA = (2 * DEPTH) ** 0.25
LN_EPS = 1e-5

HYENA_ORDER = 2
FILTER_BANDS = 16
FILTER_HIDDEN = 64
HYENA_MIN_DECAY = math.log(1e-2) / 1.5
HYENA_MAX_DECAY = math.log(1e-2) / 0.3
FILTER_NORM_EPS = 1e-6

RWKV_HEAD = 64
RWKV_HEADS = D_MODEL // RWKV_HEAD
RWKV_LORA = 64
GATE_LORA = 160
RWKV_GN_EPS = 64e-5

DA_HEAD = 64
DA_HEADS = D_MODEL // (2 * DA_HEAD)
ROPE_THETA = 10000.0
SUBLN_EPS = 1e-5

D_FF = 2816
N_EXPERTS = 8
D_FF_EXPERT = 3584

LANES = 128
VMEM_LIMIT_BYTES = 56 * 1024 * 1024


def _params(*semantics):
    return pltpu.CompilerParams(dimension_semantics=semantics, vmem_limit_bytes=VMEM_LIMIT_BYTES)


def _split_bf16(x):
    hi = x.astype(BF16)
    lo = (x - hi.astype(F32)).astype(BF16)
    return hi, lo


def _dot(a, b):
    return jnp.dot(a, b, preferred_element_type=F32)


def _dot_split(a, b):
    ah, al = _split_bf16(a)
    bh, bl = _split_bf16(b)
    return _dot(ah, bh) + _dot(al, bh) + _dot(ah, bl)


def _sigmoid(x):
    return 1.0 / (1.0 + jnp.exp(-x))


def _layer_norm(x, g, b):
    mu = jnp.mean(x, -1, keepdims=True)
    xc = x - mu
    var = jnp.mean(xc * xc, -1, keepdims=True)
    return xc * lax.rsqrt(var + LN_EPS) * g + b


def _mod_spec(which, tm, seq_len):
    return pl.BlockSpec((1, 1, D_MODEL), lambda i, *_: ((i * tm // seq_len) * 6 + which, 0, 0))


def _ada_kernel(c_ref, w_ref, b_ref, o_ref):
    c = c_ref[...]
    cs = (c * _sigmoid(c)).astype(BF16)
    o_ref[...] = _dot(cs, w_ref[...].astype(BF16)) + b_ref[...]


def ada_modulation(c, ada_w, ada_b):
    nb = c.shape[0]
    tn = 1024
    return pl.pallas_call(
        _ada_kernel,
        grid=(DEPTH, 6 * D_MODEL // tn),
        in_specs=[pl.BlockSpec((nb, D_MODEL), lambda l, j: (0, 0)),
                  pl.BlockSpec((None, D_MODEL, tn), lambda l, j: (l, 0, j)),
                  pl.BlockSpec((None, 1, tn), lambda l, j: (l, 0, j))],
        out_specs=pl.BlockSpec((None, nb, tn), lambda l, j: (l, 0, j)),
        out_shape=jax.ShapeDtypeStruct((DEPTH, nb, 6 * D_MODEL), F32),
        compiler_params=_params("parallel", "parallel"),
        name="ada_modulation",
    )(c, ada_w, ada_b.reshape(DEPTH, 1, 6 * D_MODEL))


def _proj_kernel(x_ref, sc_ref, sh_ref, w_ref, o_ref, xb_ref):
    @pl.when(pl.program_id(1) == 0)
    def _():
        xb_ref[...] = (x_ref[...] * (1.0 + sc_ref[0]) + sh_ref[0]).astype(BF16)

    o_ref[...] = _dot(xb_ref[...], w_ref[...]).astype(o_ref.dtype)


def _qkv_rope_kernel(x_ref, sc_ref, sh_ref, w_ref, cos_ref, sin_ref, o_ref, xb_ref, *, n_rope_tiles):
    j = pl.program_id(1)

    @pl.when(j == 0)
    def _():
        xb_ref[...] = (x_ref[...] * (1.0 + sc_ref[0]) + sh_ref[0]).astype(BF16)

    y = _dot(xb_ref[...], w_ref[...])
    tn = y.shape[1]

    @pl.when(j < n_rope_tiles)
    def _():
        lane = lax.broadcasted_iota(jnp.int32, y.shape, 1)
        first_half = (lane % DA_HEAD) < (DA_HEAD // 2)
        partner = jnp.where(first_half,
                            pltpu.roll(y, tn - DA_HEAD // 2, axis=1),
                            pltpu.roll(y, DA_HEAD // 2, axis=1))
        o_ref[...] = (y * cos_ref[...] + partner * sin_ref[...]).astype(o_ref.dtype)

    @pl.when(j >= n_rope_tiles)
    def _():
        o_ref[...] = y.astype(o_ref.dtype)


def modulated_projection(x, mod, w, seq_len, *, tm=1024, tn=1024):
    t_rows, n_out = x.shape[0], w.shape[1]
    return pl.pallas_call(
        _proj_kernel,
        grid=(t_rows // tm, n_out // tn),
        in_specs=[pl.BlockSpec((tm, D_MODEL), lambda i, j: (i, 0)),
                  _mod_spec(1, tm, seq_len), _mod_spec(0, tm, seq_len),
                  pl.BlockSpec((D_MODEL, tn), lambda i, j: (0, j))],
        out_specs=pl.BlockSpec((tm, tn), lambda i, j: (i, j)),
        out_shape=jax.ShapeDtypeStruct((t_rows, n_out), F32),
        scratch_shapes=[pltpu.VMEM((tm, D_MODEL), BF16)],
        compiler_params=_params("parallel", "arbitrary"),
        name="modulated_projection",
    )(x, mod, mod, w)


def qkv_rope_projection(x, mod, w, seq_len, *, tm=1024, tn=512):
    t_rows = x.shape[0]
    tiles_per_seq = seq_len // tm
    nq_tiles = D_MODEL // tn
    half = DA_HEAD // 2
    inv = ROPE_THETA ** (-jnp.arange(0, DA_HEAD, 2, dtype=F32) / DA_HEAD)
    ang = jnp.arange(seq_len, dtype=F32)[:, None] * inv[None, :]
    cos_t = jnp.tile(jnp.concatenate([jnp.cos(ang), jnp.cos(ang)], -1), (1, tn // DA_HEAD))
    sin_t = jnp.tile(jnp.concatenate([-jnp.sin(ang), jnp.sin(ang)], -1), (1, tn // DA_HEAD))
    qscale = jnp.float32(DA_HEAD ** -0.5 * math.log2(math.e))
    cos_all = jnp.concatenate([cos_t * qscale, cos_t], 0)
    sin_all = jnp.concatenate([sin_t * qscale, sin_t], 0)
    tab = lambda i, j: (jnp.where(j < nq_tiles, 0, tiles_per_seq) + i % tiles_per_seq, 0)
    return pl.pallas_call(
        functools.partial(_qkv_rope_kernel, n_rope_tiles=2 * nq_tiles),
        grid=(t_rows // tm, 3 * D_MODEL // tn),
        in_specs=[pl.BlockSpec((tm, D_MODEL), lambda i, j: (i, 0)),
                  _mod_spec(1, tm, seq_len), _mod_spec(0, tm, seq_len),
                  pl.BlockSpec((D_MODEL, tn), lambda i, j: (0, j)),
                  pl.BlockSpec((tm, tn), tab), pl.BlockSpec((tm, tn), tab)],
        out_specs=pl.BlockSpec((tm, tn), lambda i, j: (i, j)),
        out_shape=jax.ShapeDtypeStruct((t_rows, 3 * D_MODEL), BF16),
        scratch_shapes=[pltpu.VMEM((tm, D_MODEL), BF16)],
        compiler_params=_params("parallel", "arbitrary"),
        name="qkv_rope_projection",
    )(x, mod, mod, w, cos_all, sin_all)


def _out_ln_kernel(*refs, has_zgate):
    if has_zgate:
        z_ref, zg_ref, x_ref, gt_ref, w_ref, g_ref, b_ref, o_ref = refs
        z = (z_ref[...].astype(F32) * zg_ref[...]).astype(BF16)
    else:
        z_ref, x_ref, gt_ref, w_ref, g_ref, b_ref, o_ref = refs
        z = z_ref[...].astype(BF16)
    y = _dot(z, w_ref[...])
    o_ref[...] = _layer_norm(DEEPNORM_ALPHA * x_ref[...] + (1.0 + gt_ref[0]) * y, g_ref[...], b_ref[...])


def out_projection_ln(z, x, mod, w, ln_g, ln_b, seq_len, *, zgate=None, tm=512):
    t_rows = x.shape[0]
    row = pl.BlockSpec((tm, D_MODEL), lambda i: (i, 0))
    vec = pl.BlockSpec((1, D_MODEL), lambda i: (0, 0))
    in_specs = [row] + ([row] if zgate is not None else []) + [
        row, _mod_spec(2, tm, seq_len), pl.BlockSpec((D_MODEL, D_MODEL), lambda i: (0, 0)), vec, vec]
    args = [z] + ([zgate] if zgate is not None else []) + [
        x, mod, w, ln_g.reshape(1, D_MODEL), ln_b.reshape(1, D_MODEL)]
    return pl.pallas_call(
        functools.partial(_out_ln_kernel, has_zgate=zgate is not None),
        grid=(t_rows // tm,),
        in_specs=in_specs,
        out_specs=row,
        out_shape=jax.ShapeDtypeStruct((t_rows, D_MODEL), F32),
        compiler_params=_params("parallel"),
        name="out_projection_ln",
    )(*args)


def _ffn_kernel(x_ref, sc_ref, sh_ref, gt_ref, wg_ref, wu_ref, wd_ref, g_ref, b_ref, o_ref, hb_ref, acc_ref):
    f = pl.program_id(1)

    @pl.when(f == 0)
    def _():
        hb_ref[...] = (x_ref[...] * (1.0 + sc_ref[0]) + sh_ref[0]).astype(BF16)
        acc_ref[...] = jnp.zeros_like(acc_ref)

    hb = hb_ref[...]
    gate = _dot(hb, wg_ref[...])
    up = _dot(hb, wu_ref[...])
    act = (gate * _sigmoid(gate) * up).astype(BF16)
    acc_ref[...] += _dot(act, wd_ref[...])

    @pl.when(f == pl.num_programs(1) - 1)
    def _():
        o_ref[...] = _layer_norm(DEEPNORM_ALPHA * x_ref[...] + (1.0 + gt_ref[0]) * acc_ref[...],
                                 g_ref[...], b_ref[...])


def swiglu_ln(x, mod, w_gu, w_down, ln_g, ln_b, seq_len, *, tm=512, tf=1408):
    t_rows = x.shape[0]
    nf = D_FF // tf
    row = pl.BlockSpec((tm, D_MODEL), lambda i, f: (i, 0))
    vec = pl.BlockSpec((1, D_MODEL), lambda i, f: (0, 0))
    return pl.pallas_call(
        _ffn_kernel,
        grid=(t_rows // tm, nf),
        in_specs=[row, _mod_spec(4, tm, seq_len), _mod_spec(3, tm, seq_len), _mod_spec(5, tm, seq_len),
                  pl.BlockSpec((D_MODEL, tf), lambda i, f: (0, f)),
                  pl.BlockSpec((D_MODEL, tf), lambda i, f: (0, nf + f)),
                  pl.BlockSpec((tf, D_MODEL), lambda i, f: (f, 0)),
                  vec, vec],
        out_specs=row,
        out_shape=jax.ShapeDtypeStruct((t_rows, D_MODEL), F32),
        scratch_shapes=[pltpu.VMEM((tm, D_MODEL), BF16), pltpu.VMEM((tm, D_MODEL), F32)],
        compiler_params=_params("parallel", "arbitrary"),
        name="swiglu_ln",
    )(x, mod, mod, mod, w_gu, w_gu, w_down, ln_g.reshape(1, D_MODEL), ln_b.reshape(1, D_MODEL))


MOE_SUB = 128


def _moe_kernel(x_ref, sc_ref, sh_ref, gt_ref, r_ref, wg_ref, wu_ref, wd_ref, g_ref, b_ref, o_ref,
                hb_ref, comb_ref, slot_ref, slot_t_ref, cnt_ref, xg_ref, yg_ref, acc_ref):
    e = pl.program_id(1)
    f = pl.program_id(2)
    tm = x_ref.shape[0]
    sub = MOE_SUB

    @pl.when((e == 0) & (f == 0))
    def _():
        h = x_ref[...] * (1.0 + sc_ref[0]) + sh_ref[0]
        hb_ref[...] = h.astype(BF16)
        acc_ref[...] = jnp.zeros_like(acc_ref)
        logits = _dot_split(h, r_ref[...])
        lane = lax.broadcasted_iota(jnp.int32, logits.shape, 1)
        neg = jnp.float32(-jnp.inf)
        lg = jnp.where(lane < N_EXPERTS, logits, neg)
        m1 = jnp.max(lg, -1, keepdims=True)
        i1 = jnp.min(jnp.where(lg == m1, lane, LANES), -1, keepdims=True)
        lg2 = jnp.where(lane == i1, neg, lg)
        m2 = jnp.max(lg2, -1, keepdims=True)
        i2 = jnp.min(jnp.where(lg2 == m2, lane, LANES), -1, keepdims=True)
        e2 = jnp.exp(m2 - m1)
        g1 = 1.0 / (1.0 + e2)
        comb = jnp.where(lane == i1, g1, 0.0) + jnp.where(lane == i2, e2 * g1, 0.0)
        comb_ref[...] = comb
        sel = comb > 0.0
        selb = jnp.where(sel, 1.0, 0.0).astype(BF16)
        chunk = 256
        for r0 in range(0, tm, chunk):
            rr = lax.broadcasted_iota(jnp.int32, (chunk, tm), 0) + r0
            cc = lax.broadcasted_iota(jnp.int32, (chunk, tm), 1)
            before = jnp.where(cc < rr, 1.0, 0.0).astype(BF16)
            slot_ref[r0:r0 + chunk, :] = _dot(before, selb)
        slot = jnp.where(sel, slot_ref[...], -1.0)
        slot_ref[...] = slot
        slot_t_ref[...] = slot.T
        cnt_ref[...] = jnp.sum(jnp.where(sel, 1.0, 0.0), 0, keepdims=True)

    lane1 = lax.broadcasted_iota(jnp.int32, cnt_ref.shape, 1)
    n_e = jnp.sum(jnp.where(lane1 == e, cnt_ref[...], 0.0)).astype(jnp.int32)
    n_sub = (n_e + sub - 1) // sub

    @pl.when(f == 0)
    def _():
        slot_row = slot_t_ref[pl.ds(e, 1), :]

        def gather(s, carry):
            base = pl.multiple_of(s * sub, sub)
            want = (lax.broadcasted_iota(jnp.int32, (sub, tm), 0) + base).astype(F32)
            pick = jnp.where(slot_row == want, 1.0, 0.0).astype(BF16)
            xg_ref[pl.ds(base, sub), :] = _dot(pick, hb_ref[...]).astype(BF16)
            yg_ref[pl.ds(base, sub), :] = jnp.zeros((sub, D_MODEL), F32)
            return carry

        lax.fori_loop(0, n_sub, gather, 0)

    def expert_ffn(base, rows):
        xs = xg_ref[pl.ds(base, rows), :]
        gate = _dot(xs, wg_ref[...])
        up = _dot(xs, wu_ref[...])
        act = (gate * _sigmoid(gate) * up).astype(BF16)
        yg_ref[pl.ds(base, rows), :] += _dot(act, wd_ref[...])

    def ffn_pair(s, carry):
        expert_ffn(pl.multiple_of(s * (2 * sub), 2 * sub), 2 * sub)
        return carry

    lax.fori_loop(0, n_sub // 2, ffn_pair, 0)

    @pl.when(n_sub % 2 == 1)
    def _():
        expert_ffn(pl.multiple_of((n_sub - 1) * sub, sub), sub)

    @pl.when(f == pl.num_programs(2) - 1)
    def _():
        lane = lax.broadcasted_iota(jnp.int32, comb_ref.shape, 1)
        mine = lane == e
        slot_col = jnp.sum(jnp.where(mine, slot_ref[...], 0.0), -1, keepdims=True)
        c_e = jnp.sum(jnp.where(mine, comb_ref[...], 0.0), -1, keepdims=True)

        def scatter(s, carry):
            base = pl.multiple_of(s * sub, sub)
            have = (lax.broadcasted_iota(jnp.int32, (tm, sub), 1) + base).astype(F32)
            put = jnp.where(slot_col == have, 1.0, 0.0).astype(BF16)
            acc_ref[...] += c_e * _dot(put, yg_ref[pl.ds(base, sub), :].astype(BF16))
            return carry

        lax.fori_loop(0, n_sub, scatter, 0)

    @pl.when((e == pl.num_programs(1) - 1) & (f == pl.num_programs(2) - 1))
    def _():
        o_ref[...] = _layer_norm(DEEPNORM_ALPHA * x_ref[...] + (1.0 + gt_ref[0]) * acc_ref[...],
                                 g_ref[...], b_ref[...])


def moe_ln(x, mod, router, w_gu, w_down, ln_g, ln_b, seq_len, *, tm=1024, tf=896):
    t_rows = x.shape[0]
    nf = D_FF_EXPERT // tf
    router_p = jnp.pad(router, ((0, 0), (0, LANES - N_EXPERTS)))
    row = pl.BlockSpec((tm, D_MODEL), lambda i, e, f: (i, 0))
    vec = pl.BlockSpec((1, D_MODEL), lambda i, e, f: (0, 0))
    return pl.pallas_call(
        _moe_kernel,
        grid=(t_rows // tm, N_EXPERTS, nf),
        in_specs=[row, _mod_spec(4, tm, seq_len), _mod_spec(3, tm, seq_len), _mod_spec(5, tm, seq_len),
                  pl.BlockSpec((D_MODEL, LANES), lambda i, e, f: (0, 0)),
                  pl.BlockSpec((None, D_MODEL, tf), lambda i, e, f: (e, 0, f)),
                  pl.BlockSpec((None, D_MODEL, tf), lambda i, e, f: (e, 0, nf + f)),
                  pl.BlockSpec((None, tf, D_MODEL), lambda i, e, f: (e, f, 0)),
                  vec, vec],
        out_specs=row,
        out_shape=jax.ShapeDtypeStruct((t_rows, D_MODEL), F32),
        scratch_shapes=[pltpu.VMEM((tm, D_MODEL), BF16),
                        pltpu.VMEM((tm, LANES), F32),
                        pltpu.VMEM((tm, LANES), F32),
                        pltpu.VMEM((LANES, tm), F32),
                        pltpu.VMEM((1, LANES), F32),
                        pltpu.VMEM((tm, D_MODEL), BF16),
                        pltpu.VMEM((tm, D_MODEL), F32),
                        pltpu.VMEM((tm, D_MODEL), F32)],
        compiler_params=_params("parallel", "arbitrary", "arbitrary"),
        name="moe_ln",
    )(x, mod, mod, mod, router_p, w_gu, w_gu, w_down, ln_g.reshape(1, D_MODEL), ln_b.reshape(1, D_MODEL))


def _short_conv_kernel(u_ref, w_ref, b_ref, o_ref):
    u = u_ref[...]
    n = u.shape[0]
    row = lax.broadcasted_iota(jnp.int32, u.shape, 0)
    prev = jnp.where(row == 0, 0.0, pltpu.roll(u, 1, axis=0))
    nxt = jnp.where(row == n - 1, 0.0, pltpu.roll(u, n - 1, axis=0))
    o_ref[...] = prev * w_ref[0:1, :] + u * w_ref[1:2, :] + nxt * w_ref[2:3, :] + b_ref[...]


def short_conv(u, conv_w, conv_b, *, tn=256):
    nb, seq_len, width = u.shape
    blk = pl.BlockSpec((None, seq_len, tn), lambda b, j: (b, 0, j))
    return pl.pallas_call(
        _short_conv_kernel,
        grid=(nb, width // tn),
        in_specs=[blk, pl.BlockSpec((3, tn), lambda b, j: (0, j)), pl.BlockSpec((1, tn), lambda b, j: (0, j))],
        out_specs=blk,
        out_shape=jax.ShapeDtypeStruct(u.shape, F32),
        compiler_params=_params("parallel", "parallel"),
        name="short_conv",
    )(u, conv_w, conv_b.reshape(1, width))


def _filter_kernel(z_ref, w1_ref, b1_ref, w2_ref, b2_ref, w3_ref, b3_ref, w4_ref, fr_ref, dl_ref,
                   sum_ref, dif_ref, nrm_ref, nyq_ref):
    i = pl.program_id(0)
    z = z_ref[...]
    fr = fr_ref[...]
    h = jnp.sin(fr * (_dot_split(z, w1_ref[...]) + b1_ref[...]))
    h = jnp.sin(fr * (_dot_split(h, w2_ref[...]) + b2_ref[...]))
    h = jnp.sin(fr * (_dot_split(h, w3_ref[...]) + b3_ref[...]))
    hf = _dot_split(h, w4_ref[...])
    win = jnp.exp(-z[:, 0:1] * dl_ref[...])
    win = jnp.concatenate([win] * HYENA_ORDER, axis=1)
    half = HYENA_ORDER * D_MODEL
    row = lax.broadcasted_iota(jnp.int32, (z.shape[0], half), 0) + i * z.shape[0]
    kf = hf[:, :half] * win
    kb = jnp.where(row == 0, 0.0, hf[:, half:] * win)
    ksum = kf + kb
    sum_ref[...] = ksum.astype(sum_ref.dtype)
    dif_ref[...] = (kf - kb).astype(dif_ref.dtype)
    sign = (1 - 2 * (row % 2)).astype(F32)

    @pl.when(i == 0)
    def _():
        nrm_ref[...] = jnp.zeros_like(nrm_ref)
        nyq_ref[...] = jnp.zeros_like(nyq_ref)

    nrm_ref[...] += jnp.sum(jnp.abs(kf) + jnp.abs(kb), 0, keepdims=True)
    nyq_ref[...] += jnp.sum(ksum * sign, 0, keepdims=True)


def hyena_filter_taps(feat, w1p, b1, w2, b2, w3, b3, w4, freq, deltas, *, tl=256):
    seq_len = feat.shape[0]
    half = HYENA_ORDER * D_MODEL
    full = lambda a: pl.BlockSpec(a.shape, lambda i: (0,) * a.ndim)
    args = [feat, w1p, b1.reshape(1, -1), w2, b2.reshape(1, -1), w3, b3.reshape(1, -1), w4,
            freq.reshape(1, -1), deltas.reshape(1, -1)]
    return pl.pallas_call(
        _filter_kernel,
        grid=(seq_len // tl,),
        in_specs=[pl.BlockSpec((tl, LANES), lambda i: (i, 0))] + [full(a) for a in args[1:]],
        out_specs=[pl.BlockSpec((tl, half), lambda i: (i, 0)), pl.BlockSpec((tl, half), lambda i: (i, 0)),
                   pl.BlockSpec((1, half), lambda i: (0, 0)), pl.BlockSpec((1, half), lambda i: (0, 0))],
        out_shape=[jax.ShapeDtypeStruct((seq_len, half), BF16), jax.ShapeDtypeStruct((seq_len, half), BF16),
                   jax.ShapeDtypeStruct((1, half), F32), jax.ShapeDtypeStruct((1, half), F32)],
        compiler_params=_params("arbitrary"),
        name="hyena_filter_taps",
    )(*args)


def _spectrum_kernel(m_ref, k_ref, nrm_ref, nyq_ref, o_ref, acc_ref, *, patch_row0):
    kk = pl.program_id(2)

    @pl.when(kk == 0)
    def _():
        acc_ref[...] = jnp.zeros_like(acc_ref)

    acc_ref[...] += _dot(m_ref[...], k_ref[...])

    @pl.when(kk == pl.num_programs(2) - 1)
    def _():
        inv = 1.0 / (nrm_ref[...] + FILTER_NORM_EPS)
        out = acc_ref[...] * inv
        if patch_row0:
            row = lax.broadcasted_iota(jnp.int32, out.shape, 0) + pl.program_id(0) * out.shape[0]
            out = jnp.where(row == 0, nyq_ref[...] * inv, out)
        o_ref[...] = out


def filter_spectrum(mat, taps, nrm, nyq, *, patch_row0, tm=512, tn=512, tk=512):
    seq_len, width = taps.shape
    return pl.pallas_call(
        functools.partial(_spectrum_kernel, patch_row0=patch_row0),
        grid=(seq_len // tm, width // tn, seq_len // tk),
        in_specs=[pl.BlockSpec((tm, tk), lambda i, j, k: (i, k)),
                  pl.BlockSpec((tk, tn), lambda i, j, k: (k, j)),
                  pl.BlockSpec((1, tn), lambda i, j, k: (0, j)),
                  pl.BlockSpec((1, tn), lambda i, j, k: (0, j))],
        out_specs=pl.BlockSpec((tm, tn), lambda i, j, k: (i, j)),
        out_shape=jax.ShapeDtypeStruct((seq_len, width), F32),
        scratch_shapes=[pltpu.VMEM((tm, tn), F32)],
        compiler_params=_params("parallel", "parallel", "arbitrary"),
        name="filter_spectrum",
    )(mat, taps, nrm, nyq)


def _long_conv_kernel(z_ref, gate_ref, hr_ref, hi_ref, skip_ref, fc_ref, fs_ref, gc_ref, gs_ref, o_ref,
                      zb_ref, acc_ref):
    f = pl.program_id(2)

    @pl.when(f == 0)
    def _():
        zb_ref[...] = z_ref[...].astype(BF16)
        acc_ref[...] = jnp.zeros_like(acc_ref)

    zb = zb_ref[...]
    xr = _dot(fc_ref[...], zb)
    xi = _dot(fs_ref[...], zb)
    hr = hr_ref[...]
    hi = hi_ref[...]
    row = lax.broadcasted_iota(jnp.int32, xr.shape, 0)
    packed = (row == 0) & (f == 0)
    pr = jnp.where(packed, xr * hr, xr * hr - xi * hi)
    pi = jnp.where(packed, xi * hi, xr * hi + xi * hr)
    acc_ref[...] += _dot(gc_ref[...], pr.astype(BF16)) + _dot(gs_ref[...], pi.astype(BF16))

    @pl.when(f == pl.num_programs(2) - 1)
    def _():
        conv = acc_ref[...] * (1.0 / z_ref.shape[0])
        o_ref[...] = gate_ref[...] * (conv + z_ref[...] * skip_ref[...])


def long_conv_gate(zsrc, z_col0, gsrc, g_col0, spec_r, spec_i, order, skip, mats, *, tn, tf):
    nb, seq_len, _ = zsrc.shape
    fc, fs, gc, gs = mats
    nct = D_MODEL // tn
    return pl.pallas_call(
        _long_conv_kernel,
        grid=(nb, nct, seq_len // tf),
        in_specs=[pl.BlockSpec((None, seq_len, tn), lambda b, c, f: (b, 0, z_col0 + c)),
                  pl.BlockSpec((None, seq_len, tn), lambda b, c, f: (b, 0, g_col0 + c)),
                  pl.BlockSpec((tf, tn), lambda b, c, f: (f, order * nct + c)),
                  pl.BlockSpec((tf, tn), lambda b, c, f: (f, order * nct + c)),
                  pl.BlockSpec((None, 1, tn), lambda b, c, f: (order, 0, c)),
                  pl.BlockSpec((tf, seq_len), lambda b, c, f: (f, 0)),
                  pl.BlockSpec((tf, seq_len), lambda b, c, f: (f, 0)),
                  pl.BlockSpec((seq_len, tf), lambda b, c, f: (0, f)),
                  pl.BlockSpec((seq_len, tf), lambda b, c, f: (0, f))],
        out_specs=pl.BlockSpec((None, seq_len, tn), lambda b, c, f: (b, 0, c)),
        out_shape=jax.ShapeDtypeStruct((nb, seq_len, D_MODEL), F32),
        scratch_shapes=[pltpu.VMEM((seq_len, tn), BF16), pltpu.VMEM((seq_len, tn), F32)],
        compiler_params=_params("parallel", "parallel", "arbitrary"),
        name="long_conv_gate",
    )(zsrc, gsrc, spec_r, spec_i, skip.reshape(HYENA_ORDER, 1, D_MODEL), fc, fs, gc, gs)


def dft_matrices(seq_len):
    idx = jnp.arange(seq_len, dtype=jnp.int32)
    m = (idx[:, None] * idx[None, :]) % (2 * seq_len)
    ang = m.astype(F32) * (math.pi / seq_len)
    cosm = jnp.cos(ang)
    sinm = -jnp.sin(ang)
    alt = (1 - 2 * (idx % 2)).astype(F32)
    fc = cosm
    fs = sinm.at[0, :].set(alt)
    gc = cosm.at[:, 0].set(0.5)
    gs = sinm.at[:, 0].set(0.5 * alt)
    return tuple(a.astype(BF16) for a in (fc, fs, gc, gs))


def filter_features(seq_len):
    t = jnp.linspace(0.0, 1.0, seq_len, dtype=F32)[:, None]
    omega = 2.0 * math.pi * jnp.arange(seq_len, dtype=F32)[:, None] / seq_len
    bands = jnp.linspace(1e-4, FILTER_BANDS - 1, FILTER_BANDS, dtype=F32)[None, :]
    z = jnp.concatenate([t, jnp.cos(bands * omega), -jnp.sin(bands * omega)], -1)
    return jnp.pad(z, ((0, 0), (0, LANES - z.shape[1])))


def hyena_mixer(x, mod, hp, nb, seq_len):
    tn = 512 if seq_len <= 2048 else 256
    u = modulated_projection(x, mod, hp["w_in"], seq_len)
    u = short_conv(u.reshape(nb, seq_len, 3 * D_MODEL), hp["conv_w"], hp["conv_b"])
    mats = dft_matrices(seq_len)
    deltas = jnp.abs(jnp.linspace(HYENA_MIN_DECAY, HYENA_MAX_DECAY, D_MODEL, dtype=F32))
    ksum, kdif, nrm, nyq = hyena_filter_taps(filter_features(seq_len), hp["f_w1p"], hp["f_b1"], hp["f_w2"],
                                             hp["f_b2"], hp["f_w3"], hp["f_b3"], hp["f_w4"], hp["f_freq"], deltas)
    spec_r = filter_spectrum(mats[0], ksum, nrm, nyq, patch_row0=False)
    spec_i = filter_spectrum(mats[1], kdif, nrm, nyq, patch_row0=True)
    nct = D_MODEL // tn
    z = long_conv_gate(u, 0, u, nct, spec_r, spec_i, 0, hp["skip"], mats, tn=tn, tf=256)
    z = long_conv_gate(z, 0, u, 2 * nct, spec_r, spec_i, 1, hp["skip"], mats, tn=tn, tf=256)
    return z.reshape(nb * seq_len, D_MODEL)


def _rwkv_prep_kernel(x_ref, xp_ref, xn_ref, sc_ref, sh_ref, mu_ref, wrkv_ref, w1_ref, w2_ref, a1_ref, a2_ref,
                      g1_ref, g2_ref, w0_ref, a0_ref,
                      r_ref, k_ref, v_ref, dec0_ref, dec1_ref, as0_ref, as1_ref, g_ref, *, seq_len):
    i = pl.program_id(0)
    tm = x_ref.shape[0]
    sc = 1.0 + sc_ref[0]
    sh = sh_ref[0]
    h = x_ref[...] * sc + sh
    row = lax.broadcasted_iota(jnp.int32, h.shape, 0)
    pos = (row + i * tm) % seq_len
    halo_prev = xp_ref[7:8, :] * sc + sh
    halo_next = xn_ref[0:1, :] * sc + sh
    prev = jnp.where(row == 0, halo_prev, pltpu.roll(h, 1, axis=0))
    prev = jnp.where(pos == 0, 0.0, prev)
    nxt = jnp.where(row == tm - 1, halo_next, pltpu.roll(h, tm - 1, axis=0))
    nxt = jnp.where(pos == seq_len - 1, 0.0, nxt)
    xx = 0.5 * (prev + nxt) - h

    def mix(j):
        return (h + xx * mu_ref[j:j + 1, :]).astype(BF16)

    r_ref[...] = _dot(mix(0), wrkv_ref[0])
    k_ref[...] = _dot(mix(1), wrkv_ref[1])
    v_ref[...] = _dot(mix(2), wrkv_ref[2])
    lw = _dot(jnp.tanh(_dot(mix(3), w1_ref[...])).astype(BF16), w2_ref[...])
    la = _dot(_dot(mix(4), a1_ref[...]).astype(BF16), a2_ref[...])
    g_ref[...] = _dot(_sigmoid(_dot(mix(5), g1_ref[...])).astype(BF16), g2_ref[...])
    for d, (dec_ref, as_ref) in enumerate(((dec0_ref, as0_ref), (dec1_ref, as1_ref))):
        cols = slice(d * D_MODEL, (d + 1) * D_MODEL)
        pre = -(w0_ref[d:d + 1, :] + lw[:, cols])
        softplus = jnp.maximum(pre, 0.0) + jnp.log(1.0 + jnp.exp(-jnp.abs(pre)))
        dec_ref[...] = jnp.exp(-jnp.exp(-softplus - 0.5))
        as_ref[...] = _sigmoid(a0_ref[d:d + 1, :] + la[:, cols])


def rwkv_prep(x, mod, rp, seq_len, *, tm=256):
    t_rows = x.shape[0]
    row = pl.BlockSpec((tm, D_MODEL), lambda i: (i, 0))
    halo = tm // 8
    last8 = t_rows // 8 - 1
    full = lambda a: pl.BlockSpec(a.shape, lambda i: (0,) * a.ndim)
    weights = [rp["mu"], rp["w_rkv"], rp["w1"], rp["w2"], rp["a1"], rp["a2"], rp["g1"], rp["g2"], rp["w0"], rp["a0"]]
    return pl.pallas_call(
        functools.partial(_rwkv_prep_kernel, seq_len=seq_len),
        grid=(t_rows // tm,),
        in_specs=[row,
                  pl.BlockSpec((8, D_MODEL), lambda i: (jnp.maximum(i * halo - 1, 0), 0)),
                  pl.BlockSpec((8, D_MODEL), lambda i: (jnp.minimum((i + 1) * halo, last8), 0)),
                  _mod_spec(1, tm, seq_len), _mod_spec(0, tm, seq_len)] + [full(a) for a in weights],
        out_specs=[row] * 8,
        out_shape=[jax.ShapeDtypeStruct((t_rows, D_MODEL), F32)] * 8,
        compiler_params=_params("parallel"),
        name="rwkv_prep",
    )(x, x, x, mod, mod, *weights)


def _wkv_scan_kernel(*refs, reverse, tt):
    if reverse:
        (r_ref, k_ref, v_ref, w_ref, as_ref, kkp_ref, kap_ref, rkp_ref, gng_ref, gnb_ref, yin_ref, bin_ref,
         y_ref, s_ref, na_ref, b_ref, kd_ref) = refs
    else:
        (r_ref, k_ref, v_ref, w_ref, as_ref, kkp_ref, kap_ref, rkp_ref,
         y_ref, bout_ref, s_ref, na_ref, b_ref, kd_ref) = refs
    n = RWKV_HEAD

    @pl.when(pl.program_id(1) == 0)
    def _():
        s_ref[...] = jnp.zeros_like(s_ref)

    r = r_ref[...]
    k = k_ref[...]
    a_sig = as_ref[...]
    kk = k * kkp_ref[...][None]
    kk = kk / jnp.maximum(jnp.sqrt(jnp.sum(kk * kk, 1, keepdims=True)), 1e-12)
    kd = k * (1.0 + (a_sig - 1.0) * kap_ref[...][None])
    na_ref[...] = -kk
    b_ref[...] = kk * a_sig
    kd_ref[...] = kd
    bonus = jnp.sum(r * kd * rkp_ref[...][None], 1, keepdims=True)

    def time_index(j):
        return tt - 1 - j if reverse else j

    a_first = na_ref[time_index(0)]
    sa0 = jnp.zeros((n, LANES), F32)
    for kc in range(n):
        sa0 = sa0 + s_ref[kc] * a_first[kc:kc + 1, :]

    def step(j, sa):
        t = time_index(j)
        a_next = na_ref[time_index(jnp.minimum(j + 1, tt - 1))]
        w_t = w_ref[t]
        b_t = b_ref[t]
        kd_t = kd_ref[t]
        r_t = r_ref[t]
        v_t = v_ref[t]
        y = jnp.zeros((n, LANES), F32)
        sa_next = jnp.zeros((n, LANES), F32)
        for kc in range(n):
            s_new = s_ref[kc] * w_t[kc:kc + 1, :] + sa * b_t[kc:kc + 1, :] + v_t * kd_t[kc:kc + 1, :]
            s_ref[kc] = s_new
            y = y + s_new * r_t[kc:kc + 1, :]
            sa_next = sa_next + s_new * a_next[kc:kc + 1, :]
        y_ref[t] = y
        return sa_next

    lax.fori_loop(0, tt, step, sa0)

    if reverse:
        y = y_ref[...] + yin_ref[...]
        mean = jnp.mean(y, 1, keepdims=True)
        yc = y - mean
        var = jnp.mean(yc * yc, 1, keepdims=True)
        yn = yc * lax.rsqrt(var + RWKV_GN_EPS) * gng_ref[...][None] + gnb_ref[...][None]
        y_ref[...] = yn + (bonus + bin_ref[...]) * v_ref[...]
    else:
        bout_ref[...] = bonus


def wkv_scan(r, k, v, w, a_sig, chan, *, reverse, y_in=None, bonus_in=None, tt=32):
    seq_len, n, chains = r.shape
    nt = seq_len // tt
    tmap = (lambda c, j: (nt - 1 - j, 0, c)) if reverse else (lambda c, j: (j, 0, c))
    blk = pl.BlockSpec((tt, n, LANES), tmap)
    bblk = pl.BlockSpec((tt, 1, LANES), tmap)
    par = pl.BlockSpec((n, LANES), lambda c, j: (0, 0))
    in_specs = [blk] * 5 + [par] * 3
    args = [r, k, v, w, a_sig, chan["k_k"], chan["k_a"], chan["r_k"]]
    if reverse:
        in_specs += [par, par, blk, bblk]
        args += [chan["gn_g"], chan["gn_b"], y_in, bonus_in]
        out_specs = blk
        out_shape = jax.ShapeDtypeStruct(r.shape, F32)
    else:
        out_specs = [blk, bblk]
        out_shape = [jax.ShapeDtypeStruct(r.shape, F32), jax.ShapeDtypeStruct((seq_len, 1, chains), F32)]
    return pl.pallas_call(
        functools.partial(_wkv_scan_kernel, reverse=reverse, tt=tt),
        grid=(chains // LANES, nt),
        in_specs=in_specs,
        out_specs=out_specs,
        out_shape=out_shape,
        scratch_shapes=[pltpu.VMEM((n, n, LANES), F32)] + [pltpu.VMEM((tt, n, LANES), F32)] * 3,
        compiler_params=_params("parallel", "arbitrary"),
        name="wkv_scan_rev" if reverse else "wkv_scan_fwd",
    )(*args)


def rwkv_mixer(x, mod, rp, nb, seq_len):
    r, k, v, dec0, dec1, as0, as1, g = rwkv_prep(x, mod, rp, seq_len)
    chains = nb * RWKV_HEADS

    def to_chains(a):
        return a.reshape(nb, seq_len, RWKV_HEADS, RWKV_HEAD).transpose(1, 3, 0, 2).reshape(seq_len, RWKV_HEAD, chains)

    rt, kt, vt = to_chains(r), to_chains(k), to_chains(v)
    y_f, bonus_f = wkv_scan(rt, kt, vt, to_chains(dec0), to_chains(as0), rp["chan"][0], reverse=False)
    y = wkv_scan(rt, kt, vt, to_chains(dec1), to_chains(as1), rp["chan"][1], reverse=True,
                 y_in=y_f, bonus_in=bonus_f)
    y = y.reshape(seq_len, RWKV_HEAD, nb, RWKV_HEADS).transpose(2, 0, 3, 1).reshape(nb * seq_len, D_MODEL)
    return y, g


def _diff_attn_kernel(lam_ref, g_ref, q_ref, k_ref, v_ref, o_ref, *, lam_init):
    lf = lam_ref[...]
    lam = (jnp.exp(jnp.sum(lf[0:1] * lf[1:2], -1, keepdims=True))
           - jnp.exp(jnp.sum(lf[2:3] * lf[3:4], -1, keepdims=True)) + lam_init)
    q = q_ref[...]
    k = k_ref[...]

    v = v_ref[...]

    def softmax_times_v(cols):
        s = lax.dot_general(q[:, cols], k[:, cols], (((1,), (1,)), ((), ())), preferred_element_type=F32)
        e = jnp.exp2(s - jnp.max(s, -1, keepdims=True))
        return _dot(e.astype(BF16), v) * (1.0 / jnp.sum(e, -1, keepdims=True))

    o = softmax_times_v(slice(0, DA_HEAD)) - lam * softmax_times_v(slice(DA_HEAD, 2 * DA_HEAD))
    o = o * lax.rsqrt(jnp.mean(o * o, -1, keepdims=True) + SUBLN_EPS) * g_ref[...] * (1.0 - lam_init)
    o_ref[...] = o.astype(o_ref.dtype)


def diff_attention(qkv, lam, subln_g, layer_idx, nb, seq_len, *, tq=256):
    lam_init = 0.8 - 0.6 * math.exp(-0.3 * layer_idx)
    hd = 2 * DA_HEAD
    nq = seq_len // tq
    return pl.pallas_call(
        functools.partial(_diff_attn_kernel, lam_init=lam_init),
        grid=(nb, DA_HEADS, nq),
        in_specs=[pl.BlockSpec((4, DA_HEAD), lambda b, h, i: (0, 0)),
                  pl.BlockSpec((1, hd), lambda b, h, i: (0, 0)),
                  pl.BlockSpec((tq, hd), lambda b, h, i: (b * nq + i, h)),
                  pl.BlockSpec((seq_len, hd), lambda b, h, i: (b, DA_HEADS + h)),
                  pl.BlockSpec((seq_len, hd), lambda b, h, i: (b, 2 * DA_HEADS + h))],
        out_specs=pl.BlockSpec((tq, hd), lambda b, h, i: (b * nq + i, h)),
        out_shape=jax.ShapeDtypeStruct((nb * seq_len, D_MODEL), BF16),
        compiler_params=_params("parallel", "parallel", "arbitrary"),
        name="diff_attention",
    )(lam, subln_g.reshape(1, hd), qkv, qkv, qkv)


def _chan_table(p):
    t = p.reshape(RWKV_HEADS, RWKV_HEAD).T
    return jnp.tile(t, (1, LANES // RWKV_HEADS))


def _prepare_weights(P):
    bf = lambda a: a.astype(BF16)
    W = {"hy": [], "rw": [], "da": []}
    for j in range(P["hy_w_in"].shape[0]):
        W["hy"].append(dict(
            w_in=bf(P["hy_w_in"][j]), conv_w=P["hy_conv_w"][j], conv_b=P["hy_conv_b"][j],
            f_w1p=jnp.pad(P["hy_f_w1"][j], ((0, LANES - P["hy_f_w1"].shape[1]), (0, 0))),
            f_b1=P["hy_f_b1"][j], f_w2=P["hy_f_w2"][j], f_b2=P["hy_f_b2"][j], f_w3=P["hy_f_w3"][j],
            f_b3=P["hy_f_b3"][j], f_w4=P["hy_f_w4"][j], f_freq=P["hy_f_freq"][j], skip=P["hy_bias"][j],
            w_out=bf(P["hy_w_out"][j])))
    for j in range(P["rw_w_rkv"].shape[0]):
        zero = jnp.zeros((RWKV_LORA, D_MODEL), F32)
        blockdiag = lambda m: jnp.concatenate([jnp.concatenate([m[0], zero], 1), jnp.concatenate([zero, m[1]], 1)], 0)
        gpad = 2 * LANES - GATE_LORA
        W["rw"].append(dict(
            mu=P["rw_mu"][j], w_rkv=bf(P["rw_w_rkv"][j]),
            w1=bf(jnp.concatenate([P["rw_w1"][j, 0], P["rw_w1"][j, 1]], 1)), w2=bf(blockdiag(P["rw_w2"][j])),
            a1=bf(jnp.concatenate([P["rw_a1"][j, 0], P["rw_a1"][j, 1]], 1)), a2=bf(blockdiag(P["rw_a2"][j])),
            g1=bf(jnp.pad(P["rw_g1"][j], ((0, 0), (0, gpad)))), g2=bf(jnp.pad(P["rw_g2"][j], ((0, gpad), (0, 0)))),
            w0=P["rw_w0"][j], a0=P["rw_a0"][j],
            chan=[dict(k_k=_chan_table(P["rw_k_k"][j]), k_a=_chan_table(P["rw_k_a"][j]),
                       r_k=_chan_table(P["rw_r_k"][j, d]), gn_g=_chan_table(P["rw_gn_g"][j]),
                       gn_b=_chan_table(P["rw_gn_b"][j])) for d in range(2)],
            w_out=bf(P["rw_w_out"][j])))
    for j in range(P["da_w_qkv"].shape[0]):
        W["da"].append(dict(w_qkv=bf(P["da_w_qkv"][j]), lam=P["da_lam"][j], subln_g=P["da_subln_g"][j],
                            w_out=bf(P["da_w_out"][j])))
    W["ff_w_gu"], W["ff_w_down"] = bf(P["ff_w_gu"]), bf(P["ff_w_down"])
    W["moe_w_gu"], W["moe_w_down"] = bf(P["moe_w_gu"]), bf(P["moe_w_down"])
    return W


def _trunk(x, mods, P, W):
    nb, seq_len, _ = x.shape
    x = x.reshape(nb * seq_len, D_MODEL)
    for i in range(DEPTH):
        mod = mods[i].reshape(nb * 6, 1, D_MODEL)
        kind, j = i % N_MIXERS, i // N_MIXERS
        zgate = None
        if kind == 0:
            z = hyena_mixer(x, mod, W["hy"][j], nb, seq_len)
            w_out = W["hy"][j]["w_out"]
        elif kind == 1:
            z, zgate = rwkv_mixer(x, mod, W["rw"][j], nb, seq_len)
            w_out = W["rw"][j]["w_out"]
        else:
            da = W["da"][j]
            qkv = qkv_rope_projection(x, mod, da["w_qkv"], seq_len)
            z = diff_attention(qkv, da["lam"], da["subln_g"], i, nb, seq_len)
            w_out = da["w_out"]
        x = out_projection_ln(z, x, mod, w_out, P["ln_g"][i, 0], P["ln_b"][i, 0], seq_len, zgate=zgate)
        if i % 2 == 0:
            x = swiglu_ln(x, mod, W["ff_w_gu"][i // 2], W["ff_w_down"][i // 2],
                          P["ln_g"][i, 1], P["ln_b"][i, 1], seq_len)
        else:
            x = moe_ln(x, mod, P["moe_router"][i // 2], W["moe_w_gu"][i // 2], W["moe_w_down"][i // 2],
                       P["ln_g"][i, 1], P["ln_b"][i, 1], seq_len)
    return x.reshape(nb, seq_len, D_MODEL)


def kernel(x_prompt, x_sample, c_prompt, c_sample, ada_w, ada_b, ln_g, ln_b, hy_w_in, hy_conv_w, hy_conv_b, hy_f_w1, hy_f_b1, hy_f_w2, hy_f_b2, hy_f_w3, hy_f_b3, hy_f_w4, hy_f_freq, hy_bias, hy_w_out, rw_mu, rw_w_rkv, rw_w0, rw_w1, rw_w2, rw_a0, rw_a1, rw_a2, rw_g1, rw_g2, rw_k_k, rw_k_a, rw_r_k, rw_gn_g, rw_gn_b, rw_w_out, da_w_qkv, da_lam, da_subln_g, da_w_out, ff_w_gu, ff_w_down, moe_router, moe_w_gu, moe_w_down):
    P = dict(ada_w=ada_w, ada_b=ada_b, ln_g=ln_g, ln_b=ln_b,
             hy_w_in=hy_w_in, hy_conv_w=hy_conv_w, hy_conv_b=hy_conv_b,
             hy_f_w1=hy_f_w1, hy_f_b1=hy_f_b1, hy_f_w2=hy_f_w2, hy_f_b2=hy_f_b2,
             hy_f_w3=hy_f_w3, hy_f_b3=hy_f_b3, hy_f_w4=hy_f_w4, hy_f_freq=hy_f_freq,
             hy_bias=hy_bias, hy_w_out=hy_w_out,
             rw_mu=rw_mu, rw_w_rkv=rw_w_rkv, rw_w0=rw_w0, rw_w1=rw_w1, rw_w2=rw_w2,
             rw_a0=rw_a0, rw_a1=rw_a1, rw_a2=rw_a2, rw_g1=rw_g1, rw_g2=rw_g2,
             rw_k_k=rw_k_k, rw_k_a=rw_k_a, rw_r_k=rw_r_k, rw_gn_g=rw_gn_g, rw_gn_b=rw_gn_b,
             rw_w_out=rw_w_out,
             da_w_qkv=da_w_qkv, da_lam=da_lam, da_subln_g=da_subln_g, da_w_out=da_w_out,
             ff_w_gu=ff_w_gu, ff_w_down=ff_w_down,
             moe_router=moe_router, moe_w_gu=moe_w_gu, moe_w_down=moe_w_down)
    W = _prepare_weights(P)
    n_prompt = x_prompt.shape[0]
    mods = ada_modulation(jnp.concatenate([c_prompt, c_sample], 0), ada_w, ada_b)
    y_prompt = _trunk(x_prompt, mods[:, :n_prompt], P, W)
    y_sample = _trunk(x_sample, mods[:, n_prompt:], P, W)
    return (y_prompt, y_sample)
```

```python
import functools
import math

import jax
import jax.numpy as jnp
from jax import lax
from jax.experimental import pallas as pl
from jax.experimental.pallas import tpu as pltpu

F32 = jnp.float32
BF16 = jnp.bfloat16

D_MODEL = 1024
DEPTH = 4
N_MIXERS = 3
DEEPNORM_ALPHA = (2 * DEPTH) ** 0.25
LN_EPS = 1e-5

HYENA_ORDER = 2
FILTER_BANDS = 16
FILTER_HIDDEN = 64
HYENA_MIN_DECAY = math.log(1e-2) / 1.5
HYENA_MAX_DECAY = math.log(1e-2) / 0.3
FILTER_NORM_EPS = 1e-6

RWKV_HEAD = 64
RWKV_HEADS = D_MODEL // RWKV_HEAD
RWKV_LORA = 64
GATE_LORA = 160
RWKV_GN_EPS = 64e-5

DA_HEAD = 64
DA_HEADS = D_MODEL // (2 * DA_HEAD)
ROPE_THETA = 10000.0
SUBLN_EPS = 1e-5

D_FF = 2816
N_EXPERTS = 8
D_FF_EXPERT = 3584

LANES = 128
VMEM_LIMIT_BYTES = 56 * 1024 * 1024


def _params(*semantics):
    return pltpu.CompilerParams(dimension_semantics=semantics, vmem_limit_bytes=VMEM_LIMIT_BYTES)


def _split_bf16(x):
    hi = x.astype(BF16)
    lo = (x - hi.astype(F32)).astype(BF16)
    return hi, lo


def _dot(a, b):
    return jnp.dot(a, b, preferred_element_type=F32)


def _dot_split(a, b):
    ah, al = _split_bf16(a)
    bh, bl = _split_bf16(b)
    return _dot(ah, bh) + _dot(al, bh) + _dot(ah, bl)


def _sigmoid(x):
    return 1.0 / (1.0 + jnp.exp(-x))


def _layer_norm(x, g, b):
    mu = jnp.mean(x, -1, keepdims=True)
    xc = x - mu
    var = jnp.mean(xc * xc, -1, keepdims=True)
    return xc * lax.rsqrt(var + LN_EPS) * g + b


def _mod_spec(which, tm, seq_len):
    return pl.BlockSpec((1, 1, D_MODEL), lambda i, *_: ((i * tm // seq_len) * 6 + which, 0, 0))


def _ada_kernel(c_ref, w_ref, b_ref, o_ref):
    c = c_ref[...]
    cs = (c * _sigmoid(c)).astype(BF16)
    o_ref[...] = _dot(cs, w_ref[...].astype(BF16)) + b_ref[...]


def ada_modulation(c, ada_w, ada_b):
    nb = c.shape[0]
    tn = 1024
    return pl.pallas_call(
        _ada_kernel,
        grid=(DEPTH, 6 * D_MODEL // tn),
        in_specs=[pl.BlockSpec((nb, D_MODEL), lambda l, j: (0, 0)),
                  pl.BlockSpec((None, D_MODEL, tn), lambda l, j: (l, 0, j)),
                  pl.BlockSpec((None, 1, tn), lambda l, j: (l, 0, j))],
        out_specs=pl.BlockSpec((None, nb, tn), lambda l, j: (l, 0, j)),
        out_shape=jax.ShapeDtypeStruct((DEPTH, nb, 6 * D_MODEL), F32),
        compiler_params=_params("parallel", "parallel"),
        name="ada_modulation",
    )(c, ada_w, ada_b.reshape(DEPTH, 1, 6 * D_MODEL))


def _proj_kernel(x_ref, sc_ref, sh_ref, w_ref, o_ref, xb_ref):
    @pl.when(pl.program_id(1) == 0)
    def _():
        xb_ref[...] = (x_ref[...] * (1.0 + sc_ref[0]) + sh_ref[0]).astype(BF16)

    o_ref[...] = _dot(xb_ref[...], w_ref[...]).astype(o_ref.dtype)


def _qkv_rope_kernel(x_ref, sc_ref, sh_ref, w_ref, cos_ref, sin_ref, o_ref, xb_ref, *, n_rope_tiles):
    j = pl.program_id(1)

    @pl.when(j == 0)
    def _():
        xb_ref[...] = (x_ref[...] * (1.0 + sc_ref[0]) + sh_ref[0]).astype(BF16)

    y = _dot(xb_ref[...], w_ref[...])
    tn = y.shape[1]

    @pl.when(j < n_rope_tiles)
    def _():
        lane = lax.broadcasted_iota(jnp.int32, y.shape, 1)
        first_half = (lane % DA_HEAD) < (DA_HEAD // 2)
        partner = jnp.where(first_half,
                            pltpu.roll(y, tn - DA_HEAD // 2, axis=1),
                            pltpu.roll(y, DA_HEAD // 2, axis=1))
        o_ref[...] = (y * cos_ref[...] + partner * sin_ref[...]).astype(o_ref.dtype)

    @pl.when(j >= n_rope_tiles)
    def _():
        o_ref[...] = y.astype(o_ref.dtype)


def modulated_projection(x, mod, w, seq_len, *, tm=1024, tn=1024):
    t_rows, n_out = x.shape[0], w.shape[1]
    return pl.pallas_call(
        _proj_kernel,
        grid=(t_rows // tm, n_out // tn),
        in_specs=[pl.BlockSpec((tm, D_MODEL), lambda i, j: (i, 0)),
                  _mod_spec(1, tm, seq_len), _mod_spec(0, tm, seq_len),
                  pl.BlockSpec((D_MODEL, tn), lambda i, j: (0, j))],
        out_specs=pl.BlockSpec((tm, tn), lambda i, j: (i, j)),
        out_shape=jax.ShapeDtypeStruct((t_rows, n_out), F32),
        scratch_shapes=[pltpu.VMEM((tm, D_MODEL), BF16)],
        compiler_params=_params("parallel", "arbitrary"),
        name="modulated_projection",
    )(x, mod, mod, w)


def qkv_rope_projection(x, mod, w, seq_len, *, tm=1024, tn=512):
    t_rows = x.shape[0]
    tiles_per_seq = seq_len // tm
    nq_tiles = D_MODEL // tn
    half = DA_HEAD // 2
    inv = ROPE_THETA ** (-jnp.arange(0, DA_HEAD, 2, dtype=F32) / DA_HEAD)
    ang = jnp.arange(seq_len, dtype=F32)[:, None] * inv[None, :]
    cos_t = jnp.tile(jnp.concatenate([jnp.cos(ang), jnp.cos(ang)], -1), (1, tn // DA_HEAD))
    sin_t = jnp.tile(jnp.concatenate([-jnp.sin(ang), jnp.sin(ang)], -1), (1, tn // DA_HEAD))
    qscale = jnp.float32(DA_HEAD ** -0.5 * math.log2(math.e))
    cos_all = jnp.concatenate([cos_t * qscale, cos_t], 0)
    sin_all = jnp.concatenate([sin_t * qscale, sin_t], 0)
    tab = lambda i, j: (jnp.where(j < nq_tiles, 0, tiles_per_seq) + i % tiles_per_seq, 0)
    return pl.pallas_call(
        functools.partial(_qkv_rope_kernel, n_rope_tiles=2 * nq_tiles),
        grid=(t_rows // tm, 3 * D_MODEL // tn),
        in_specs=[pl.BlockSpec((tm, D_MODEL), lambda i, j: (i, 0)),
                  _mod_spec(1, tm, seq_len), _mod_spec(0, tm, seq_len),
                  pl.BlockSpec((D_MODEL, tn), lambda i, j: (0, j)),
                  pl.BlockSpec((tm, tn), tab), pl.BlockSpec((tm, tn), tab)],
        out_specs=pl.BlockSpec((tm, tn), lambda i, j: (i, j)),
        out_shape=jax.ShapeDtypeStruct((t_rows, 3 * D_MODEL), BF16),
        scratch_shapes=[pltpu.VMEM((tm, D_MODEL), BF16)],
        compiler_params=_params("parallel", "arbitrary"),
        name="qkv_rope_projection",
    )(x, mod, mod, w, cos_all, sin_all)


def _out_ln_kernel(*refs, has_zgate):
    if has_zgate:
        z_ref, zg_ref, x_ref, gt_ref, w_ref, g_ref, b_ref, o_ref = refs
        z = (z_ref[...].astype(F32) * zg_ref[...]).astype(BF16)
    else:
        z_ref, x_ref, gt_ref, w_ref, g_ref, b_ref, o_ref = refs
        z = z_ref[...].astype(BF16)
    y = _dot(z, w_ref[...])
    o_ref[...] = _layer_norm(DEEPNORM_ALPHA * x_ref[...] + (1.0 + gt_ref[0]) * y, g_ref[...], b_ref[...])


def out_projection_ln(z, x, mod, w, ln_g, ln_b, seq_len, *, zgate=None, tm=512):
    t_rows = x.shape[0]
    row = pl.BlockSpec((tm, D_MODEL), lambda i: (i, 0))
    vec = pl.BlockSpec((1, D_MODEL), lambda i: (0, 0))
    in_specs = [row] + ([row] if zgate is not None else []) + [
        row, _mod_spec(2, tm, seq_len), pl.BlockSpec((D_MODEL, D_MODEL), lambda i: (0, 0)), vec, vec]
    args = [z] + ([zgate] if zgate is not None else []) + [
        x, mod, w, ln_g.reshape(1, D_MODEL), ln_b.reshape(1, D_MODEL)]
    return pl.pallas_call(
        functools.partial(_out_ln_kernel, has_zgate=zgate is not None),
        grid=(t_rows // tm,),
        in_specs=in_specs,
        out_specs=row,
        out_shape=jax.ShapeDtypeStruct((t_rows, D_MODEL), F32),
        compiler_params=_params("parallel"),
        name="out_projection_ln",
    )(*args)


def _ffn_kernel(x_ref, sc_ref, sh_ref, gt_ref, wg_ref, wu_ref, wd_ref, g_ref, b_ref, o_ref, hb_ref, acc_ref):
    f = pl.program_id(1)

    @pl.when(f == 0)
    def _():
        hb_ref[...] = (x_ref[...] * (1.0 + sc_ref[0]) + sh_ref[0]).astype(BF16)
        acc_ref[...] = jnp.zeros_like(acc_ref)

    hb = hb_ref[...]
    gate = _dot(hb, wg_ref[...])
    up = _dot(hb, wu_ref[...])
    act = (gate * _sigmoid(gate) * up).astype(BF16)
    acc_ref[...] += _dot(act, wd_ref[...])

    @pl.when(f == pl.num_programs(1) - 1)
    def _():
        o_ref[...] = _layer_norm(DEEPNORM_ALPHA * x_ref[...] + (1.0 + gt_ref[0]) * acc_ref[...],
                                 g_ref[...], b_ref[...])


def swiglu_ln(x, mod, w_gu, w_down, ln_g, ln_b, seq_len, *, tm=512, tf=1408):
    t_rows = x.shape[0]
    nf = D_FF // tf
    row = pl.BlockSpec((tm, D_MODEL), lambda i, f: (i, 0))
    vec = pl.BlockSpec((1, D_MODEL), lambda i, f: (0, 0))
    return pl.pallas_call(
        _ffn_kernel,
        grid=(t_rows // tm, nf),
        in_specs=[row, _mod_spec(4, tm, seq_len), _mod_spec(3, tm, seq_len), _mod_spec(5, tm, seq_len),
                  pl.BlockSpec((D_MODEL, tf), lambda i, f: (0, f)),
                  pl.BlockSpec((D_MODEL, tf), lambda i, f: (0, nf + f)),
                  pl.BlockSpec((tf, D_MODEL), lambda i, f: (f, 0)),
                  vec, vec],
        out_specs=row,
        out_shape=jax.ShapeDtypeStruct((t_rows, D_MODEL), F32),
        scratch_shapes=[pltpu.VMEM((tm, D_MODEL), BF16), pltpu.VMEM((tm, D_MODEL), F32)],
        compiler_params=_params("parallel", "arbitrary"),
        name="swiglu_ln",
    )(x, mod, mod, mod, w_gu, w_gu, w_down, ln_g.reshape(1, D_MODEL), ln_b.reshape(1, D_MODEL))


MOE_SUB = 144


def _moe_kernel(x_ref, sc_ref, sh_ref, gt_ref, r_ref, wg_ref, wu_ref, wd_ref, g_ref, b_ref, o_ref,
                hb_ref, comb_ref, slot_ref, slot_t_ref, cnt_ref, xg_ref, yg_ref, acc_ref):
    e = pl.program_id(1)
    f = pl.program_id(2)
    tm = x_ref.shape[0]
    sub = MOE_SUB

    @pl.when((e == 0) & (f == 0))
    def _():
        h = x_ref[...] * (1.0 + sc_ref[0]) + sh_ref[0]
        hb_ref[...] = h.astype(BF16)
        acc_ref[...] = jnp.zeros_like(acc_ref)
        logits = _dot_split(h, r_ref[...])
        lane = lax.broadcasted_iota(jnp.int32, logits.shape, 1)
        neg = jnp.float32(-jnp.inf)
        lg = jnp.where(lane < N_EXPERTS, logits, neg)
        m1 = jnp.max(lg, -1, keepdims=True)
        i1 = jnp.min(jnp.where(lg == m1, lane, LANES), -1, keepdims=True)
        lg2 = jnp.where(lane == i1, neg, lg)
        m2 = jnp.max(lg2, -1, keepdims=True)
        i2 = jnp.min(jnp.where(lg2 == m2, lane, LANES), -1, keepdims=True)
        e2 = jnp.exp(m2 - m1)
        g1 = 1.0 / (1.0 + e2)
        comb = jnp.where(lane == i1, g1, 0.0) + jnp.where(lane == i2, e2 * g1, 0.0)
        comb_ref[...] = comb
        sel = comb > 0.0
        selb = jnp.where(sel, 1.0, 0.0).astype(BF16)
        chunk = 256
        for r0 in range(0, tm, chunk):
            rr = lax.broadcasted_iota(jnp.int32, (chunk, tm), 0) + r0
            cc = lax.broadcasted_iota(jnp.int32, (chunk, tm), 1)
            before = jnp.where(cc < rr, 1.0, 0.0).astype(BF16)
            slot_ref[r0:r0 + chunk, :] = _dot(before, selb)
        slot = jnp.where(sel, slot_ref[...], -1.0)
        slot_ref[...] = slot
        slot_t_ref[...] = slot.T
        cnt_ref[...] = jnp.sum(jnp.where(sel, 1.0, 0.0), 0, keepdims=True)

    lane1 = lax.broadcasted_iota(jnp.int32, cnt_ref.shape, 1)
    n_e = jnp.sum(jnp.where(lane1 == e, cnt_ref[...], 0.0)).astype(jnp.int32)
    n_sub = (n_e + sub - 1) // sub

    @pl.when(f == 0)
    def _():
        slot_row = slot_t_ref[pl.ds(e, 1), :]

        def gather(s, carry):
            base = pl.multiple_of(s * sub, sub)
            want = (lax.broadcasted_iota(jnp.int32, (sub, tm), 0) + base).astype(F32)
            pick = jnp.where(slot_row == want, 1.0, 0.0).astype(BF16)
            xg_ref[pl.ds(base, sub), :] = _dot(pick, hb_ref[...]).astype(BF16)
            yg_ref[pl.ds(base, sub), :] = jnp.zeros((sub, D_MODEL), F32)
            return carry

        lax.fori_loop(0, n_sub, gather, 0)

    def expert_ffn(base, rows):
        xs = xg_ref[pl.ds(base, rows), :]
        gate = _dot(xs, wg_ref[...])
        up = _dot(xs, wu_ref[...])
        act = (gate * _sigmoid(gate) * up).astype(BF16)
        yg_ref[pl.ds(base, rows), :] += _dot(act, wd_ref[...])

    def ffn_pair(s, carry):
        expert_ffn(pl.multiple_of(s * (2 * sub), 2 * sub), 2 * sub)
        return carry

    lax.fori_loop(0, n_sub // 2, ffn_pair, 0)

    @pl.when(n_sub % 2 == 1)
    def _():
        expert_ffn(pl.multiple_of((n_sub - 1) * sub, sub), sub)

    @pl.when(f == pl.num_programs(2) - 1)
    def _():
        lane = lax.broadcasted_iota(jnp.int32, comb_ref.shape, 1)
        mine = lane == e
        slot_col = jnp.sum(jnp.where(mine, slot_ref[...], 0.0), -1, keepdims=True)
        c_e = jnp.sum(jnp.where(mine, comb_ref[...], 0.0), -1, keepdims=True)

        def scatter(s, carry):
            base = pl.multiple_of(s * sub, sub)
            have = (lax.broadcasted_iota(jnp.int32, (tm, sub), 1) + base).astype(F32)
            put = jnp.where(slot_col == have, 1.0, 0.0).astype(BF16)
            acc_ref[...] += c_e * _dot(put, yg_ref[pl.ds(base, sub), :].astype(BF16))
            return carry

        lax.fori_loop(0, n_sub, scatter, 0)

    @pl.when((e == pl.num_programs(1) - 1) & (f == pl.num_programs(2) - 1))
    def _():
        o_ref[...] = _layer_norm(DEEPNORM_ALPHA * x_ref[...] + (1.0 + gt_ref[0]) * acc_ref[...],
                                 g_ref[...], b_ref[...])


def moe_ln(x, mod, router, w_gu, w_down, ln_g, ln_b, seq_len, *, tm=1024, tf=896):
    t_rows = x.shape[0]
    nf = D_FF_EXPERT // tf
    router_p = jnp.pad(router, ((0, 0), (0, LANES - N_EXPERTS)))
    slots = -(-tm // MOE_SUB) * MOE_SUB
    row = pl.BlockSpec((tm, D_MODEL), lambda i, e, f: (i, 0))
    vec = pl.BlockSpec((1, D_MODEL), lambda i, e, f: (0, 0))
    return pl.pallas_call(
        _moe_kernel,
        grid=(t_rows // tm, N_EXPERTS, nf),
        in_specs=[row, _mod_spec(4, tm, seq_len), _mod_spec(3, tm, seq_len), _mod_spec(5, tm, seq_len),
                  pl.BlockSpec((D_MODEL, LANES), lambda i, e, f: (0, 0)),
                  pl.BlockSpec((None, D_MODEL, tf), lambda i, e, f: (e, 0, f)),
                  pl.BlockSpec((None, D_MODEL, tf), lambda i, e, f: (e, 0, nf + f)),
                  pl.BlockSpec((None, tf, D_MODEL), lambda i, e, f: (e, f, 0)),
                  vec, vec],
        out_specs=row,
        out_shape=jax.ShapeDtypeStruct((t_rows, D_MODEL), F32),
        scratch_shapes=[pltpu.VMEM((tm, D_MODEL), BF16),
                        pltpu.VMEM((tm, LANES), F32),
                        pltpu.VMEM((tm, LANES), F32),
                        pltpu.VMEM((LANES, tm), F32),
                        pltpu.VMEM((1, LANES), F32),
                        pltpu.VMEM((slots, D_MODEL), BF16),
                        pltpu.VMEM((slots, D_MODEL), F32),
                        pltpu.VMEM((tm, D_MODEL), F32)],
        compiler_params=_params("parallel", "arbitrary", "arbitrary"),
        name="moe_ln",
    )(x, mod, mod, mod, router_p, w_gu, w_gu, w_down, ln_g.reshape(1, D_MODEL), ln_b.reshape(1, D_MODEL))


def _short_conv_kernel(u_ref, w_ref, b_ref, o_ref):
    u = u_ref[...]
    n = u.shape[0]
    row = lax.broadcasted_iota(jnp.int32, u.shape, 0)
    prev = jnp.where(row == 0, 0.0, pltpu.roll(u, 1, axis=0))
    nxt = jnp.where(row == n - 1, 0.0, pltpu.roll(u, n - 1, axis=0))
    o_ref[...] = prev * w_ref[0:1, :] + u * w_ref[1:2, :] + nxt * w_ref[2:3, :] + b_ref[...]


def short_conv(u, conv_w, conv_b, *, tn=256):
    nb, seq_len, width = u.shape
    blk = pl.BlockSpec((None, seq_len, tn), lambda b, j: (b, 0, j))
    return pl.pallas_call(
        _short_conv_kernel,
        grid=(nb, width // tn),
        in_specs=[blk, pl.BlockSpec((3, tn), lambda b, j: (0, j)), pl.BlockSpec((1, tn), lambda b, j: (0, j))],
        out_specs=blk,
        out_shape=jax.ShapeDtypeStruct(u.shape, F32),
        compiler_params=_params("parallel", "parallel"),
        name="short_conv",
    )(u, conv_w, conv_b.reshape(1, width))


def _filter_kernel(z_ref, w1_ref, b1_ref, w2_ref, b2_ref, w3_ref, b3_ref, w4_ref, fr_ref, dl_ref,
                   sum_ref, dif_ref, nrm_ref, nyq_ref):
    i = pl.program_id(0)
    z = z_ref[...]
    fr = fr_ref[...]
    h = jnp.sin(fr * (_dot_split(z, w1_ref[...]) + b1_ref[...]))
    h = jnp.sin(fr * (_dot_split(h, w2_ref[...]) + b2_ref[...]))
    h = jnp.sin(fr * (_dot_split(h, w3_ref[...]) + b3_ref[...]))
    hf = _dot_split(h, w4_ref[...])
    win = jnp.exp(-z[:, 0:1] * dl_ref[...])
    win = jnp.concatenate([win] * HYENA_ORDER, axis=1)
    half = HYENA_ORDER * D_MODEL
    row = lax.broadcasted_iota(jnp.int32, (z.shape[0], half), 0) + i * z.shape[0]
    kf = hf[:, :half] * win
    kb = jnp.where(row == 0, 0.0, hf[:, half:] * win)
    ksum = kf + kb
    sum_ref[...] = ksum.astype(sum_ref.dtype)
    dif_ref[...] = (kf - kb).astype(dif_ref.dtype)
    sign = (1 - 2 * (row % 2)).astype(F32)

    @pl.when(i == 0)
    def _():
        nrm_ref[...] = jnp.zeros_like(nrm_ref)
        nyq_ref[...] = jnp.zeros_like(nyq_ref)

    nrm_ref[...] += jnp.sum(jnp.abs(kf) + jnp.abs(kb), 0, keepdims=True)
    nyq_ref[...] += jnp.sum(ksum * sign, 0, keepdims=True)


def hyena_filter_taps(feat, w1p, b1, w2, b2, w3, b3, w4, freq, deltas, *, tl=256):
    seq_len = feat.shape[0]
    half = HYENA_ORDER * D_MODEL
    full = lambda a: pl.BlockSpec(a.shape, lambda i: (0,) * a.ndim)
    args = [feat, w1p, b1.reshape(1, -1), w2, b2.reshape(1, -1), w3, b3.reshape(1, -1), w4,
            freq.reshape(1, -1), deltas.reshape(1, -1)]
    return pl.pallas_call(
        _filter_kernel,
        grid=(seq_len // tl,),
        in_specs=[pl.BlockSpec((tl, LANES), lambda i: (i, 0))] + [full(a) for a in args[1:]],
        out_specs=[pl.BlockSpec((tl, half), lambda i: (i, 0)), pl.BlockSpec((tl, half), lambda i: (i, 0)),
                   pl.BlockSpec((1, half), lambda i: (0, 0)), pl.BlockSpec((1, half), lambda i: (0, 0))],
        out_shape=[jax.ShapeDtypeStruct((seq_len, half), BF16), jax.ShapeDtypeStruct((seq_len, half), BF16),
                   jax.ShapeDtypeStruct((1, half), F32), jax.ShapeDtypeStruct((1, half), F32)],
        compiler_params=_params("arbitrary"),
        name="hyena_filter_taps",
    )(*args)


def _spectrum_kernel(m_ref, k_ref, nrm_ref, nyq_ref, o_ref, acc_ref, *, patch_row0):
    kk = pl.program_id(2)

    @pl.when(kk == 0)
    def _():
        acc_ref[...] = jnp.zeros_like(acc_ref)

    acc_ref[...] += _dot(m_ref[...], k_ref[...])

    @pl.when(kk == pl.num_programs(2) - 1)
    def _():
        inv = 1.0 / (nrm_ref[...] + FILTER_NORM_EPS)
        out = acc_ref[...] * inv
        if patch_row0:
            row = lax.broadcasted_iota(jnp.int32, out.shape, 0) + pl.program_id(0) * out.shape[0]
            out = jnp.where(row == 0, nyq_ref[...] * inv, out)
        o_ref[...] = out


def filter_spectrum(mat, taps, nrm, nyq, *, patch_row0, tm=512, tn=512, tk=512):
    seq_len, width = taps.shape
    return pl.pallas_call(
        functools.partial(_spectrum_kernel, patch_row0=patch_row0),
        grid=(seq_len // tm, width // tn, seq_len // tk),
        in_specs=[pl.BlockSpec((tm, tk), lambda i, j, k: (i, k)),
                  pl.BlockSpec((tk, tn), lambda i, j, k: (k, j)),
                  pl.BlockSpec((1, tn), lambda i, j, k: (0, j)),
                  pl.BlockSpec((1, tn), lambda i, j, k: (0, j))],
        out_specs=pl.BlockSpec((tm, tn), lambda i, j, k: (i, j)),
        out_shape=jax.ShapeDtypeStruct((seq_len, width), F32),
        scratch_shapes=[pltpu.VMEM((tm, tn), F32)],
        compiler_params=_params("parallel", "parallel", "arbitrary"),
        name="filter_spectrum",
    )(mat, taps, nrm, nyq)


def _long_conv_kernel(z_ref, gate_ref, hr_ref, hi_ref, skip_ref, fc_ref, fs_ref, gc_ref, gs_ref, o_ref,
                      zb_ref, acc_ref):
    f = pl.program_id(2)

    @pl.when(f == 0)
    def _():
        zb_ref[...] = z_ref[...].astype(BF16)
        acc_ref[...] = jnp.zeros_like(acc_ref)

    zb = zb_ref[...]
    xr = _dot(fc_ref[...], zb)
    xi = _dot(fs_ref[...], zb)
    hr = hr_ref[...]
    hi = hi_ref[...]
    row = lax.broadcasted_iota(jnp.int32, xr.shape, 0)
    packed = (row == 0) & (f == 0)
    pr = jnp.where(packed, xr * hr, xr * hr - xi * hi)
    pi = jnp.where(packed, xi * hi, xr * hi + xi * hr)
    acc_ref[...] += _dot(gc_ref[...], pr.astype(BF16)) + _dot(gs_ref[...], pi.astype(BF16))

    @pl.when(f == pl.num_programs(2) - 1)
    def _():
        conv = acc_ref[...] * (1.0 / z_ref.shape[0])
        o_ref[...] = gate_ref[...] * (conv + z_ref[...] * skip_ref[...])


def long_conv_gate(zsrc, z_col0, gsrc, g_col0, spec_r, spec_i, order, skip, mats, *, tn, tf):
    nb, seq_len, _ = zsrc.shape
    fc, fs, gc, gs = mats
    nct = D_MODEL // tn
    return pl.pallas_call(
        _long_conv_kernel,
        grid=(nb, nct, seq_len // tf),
        in_specs=[pl.BlockSpec((None, seq_len, tn), lambda b, c, f: (b, 0, z_col0 + c)),
                  pl.BlockSpec((None, seq_len, tn), lambda b, c, f: (b, 0, g_col0 + c)),
                  pl.BlockSpec((tf, tn), lambda b, c, f: (f, order * nct + c)),
                  pl.BlockSpec((tf, tn), lambda b, c, f: (f, order * nct + c)),
                  pl.BlockSpec((None, 1, tn), lambda b, c, f: (order, 0, c)),
                  pl.BlockSpec((tf, seq_len), lambda b, c, f: (f, 0)),
                  pl.BlockSpec((tf, seq_len), lambda b, c, f: (f, 0)),
                  pl.BlockSpec((seq_len, tf), lambda b, c, f: (0, f)),
                  pl.BlockSpec((seq_len, tf), lambda b, c, f: (0, f))],
        out_specs=pl.BlockSpec((None, seq_len, tn), lambda b, c, f: (b, 0, c)),
        out_shape=jax.ShapeDtypeStruct((nb, seq_len, D_MODEL), F32),
        scratch_shapes=[pltpu.VMEM((seq_len, tn), BF16), pltpu.VMEM((seq_len, tn), F32)],
        compiler_params=_params("parallel", "parallel", "arbitrary"),
        name="long_conv_gate",
    )(zsrc, gsrc, spec_r, spec_i, skip.reshape(HYENA_ORDER, 1, D_MODEL), fc, fs, gc, gs)


def dft_matrices(seq_len):
    idx = jnp.arange(seq_len, dtype=jnp.int32)
    m = (idx[:, None] * idx[None, :]) % (2 * seq_len)
    ang = m.astype(F32) * (math.pi / seq_len)
    cosm = jnp.cos(ang)
    sinm = -jnp.sin(ang)
    alt = (1 - 2 * (idx % 2)).astype(F32)
    fc = cosm
    fs = sinm.at[0, :].set(alt)
    gc = cosm.at[:, 0].set(0.5)
    gs = sinm.at[:, 0].set(0.5 * alt)
    return tuple(a.astype(BF16) for a in (fc, fs, gc, gs))


def filter_features(seq_len):
    t = jnp.linspace(0.0, 1.0, seq_len, dtype=F32)[:, None]
    omega = 2.0 * math.pi * jnp.arange(seq_len, dtype=F32)[:, None] / seq_len
    bands = jnp.linspace(1e-4, FILTER_BANDS - 1, FILTER_BANDS, dtype=F32)[None, :]
    z = jnp.concatenate([t, jnp.cos(bands * omega), -jnp.sin(bands * omega)], -1)
    return jnp.pad(z, ((0, 0), (0, LANES - z.shape[1])))


def hyena_mixer(x, mod, hp, nb, seq_len):
    tn = 512 if seq_len <= 2048 else 256
    u = modulated_projection(x, mod, hp["w_in"], seq_len)
    u = short_conv(u.reshape(nb, seq_len, 3 * D_MODEL), hp["conv_w"], hp["conv_b"])
    mats = dft_matrices(seq_len)
    deltas = jnp.abs(jnp.linspace(HYENA_MIN_DECAY, HYENA_MAX_DECAY, D_MODEL, dtype=F32))
    ksum, kdif, nrm, nyq = hyena_filter_taps(filter_features(seq_len), hp["f_w1p"], hp["f_b1"], hp["f_w2"],
                                             hp["f_b2"], hp["f_w3"], hp["f_b3"], hp["f_w4"], hp["f_freq"], deltas)
    spec_r = filter_spectrum(mats[0], ksum, nrm, nyq, patch_row0=False)
    spec_i = filter_spectrum(mats[1], kdif, nrm, nyq, patch_row0=True)
    nct = D_MODEL // tn
    z = long_conv_gate(u, 0, u, nct, spec_r, spec_i, 0, hp["skip"], mats, tn=tn, tf=256)
    z = long_conv_gate(z, 0, u, 2 * nct, spec_r, spec_i, 1, hp["skip"], mats, tn=tn, tf=256)
    return z.reshape(nb * seq_len, D_MODEL)


def _rwkv_prep_kernel(x_ref, xp_ref, xn_ref, sc_ref, sh_ref, mu_ref, wrkv_ref, w1_ref, w2_ref, a1_ref, a2_ref,
                      g1_ref, g2_ref, w0_ref, a0_ref,
                      r_ref, k_ref, v_ref, dec0_ref, dec1_ref, as0_ref, as1_ref, g_ref, *, seq_len):
    i = pl.program_id(0)
    tm = x_ref.shape[0]
    sc = 1.0 + sc_ref[0]
    sh = sh_ref[0]
    h = x_ref[...] * sc + sh
    row = lax.broadcasted_iota(jnp.int32, h.shape, 0)
    pos = (row + i * tm) % seq_len
    halo_prev = xp_ref[7:8, :] * sc + sh
    halo_next = xn_ref[0:1, :] * sc + sh
    prev = jnp.where(row == 0, halo_prev, pltpu.roll(h, 1, axis=0))
    prev = jnp.where(pos == 0, 0.0, prev)
    nxt = jnp.where(row == tm - 1, halo_next, pltpu.roll(h, tm - 1, axis=0))
    nxt = jnp.where(pos == seq_len - 1, 0.0, nxt)
    xx = 0.5 * (prev + nxt) - h

    def mix(j):
        return (h + xx * mu_ref[j:j + 1, :]).astype(BF16)

    r_ref[...] = _dot(mix(0), wrkv_ref[0])
    k_ref[...] = _dot(mix(1), wrkv_ref[1])
    v_ref[...] = _dot(mix(2), wrkv_ref[2])
    lw = _dot(jnp.tanh(_dot(mix(3), w1_ref[...])).astype(BF16), w2_ref[...])
    la = _dot(_dot(mix(4), a1_ref[...]).astype(BF16), a2_ref[...])
    g_ref[...] = _dot(_sigmoid(_dot(mix(5), g1_ref[...])).astype(BF16), g2_ref[...])
    for d, (dec_ref, as_ref) in enumerate(((dec0_ref, as0_ref), (dec1_ref, as1_ref))):
        cols = slice(d * D_MODEL, (d + 1) * D_MODEL)
        pre = -(w0_ref[d:d + 1, :] + lw[:, cols])
        softplus = jnp.maximum(pre, 0.0) + jnp.log(1.0 + jnp.exp(-jnp.abs(pre)))
        dec_ref[...] = jnp.exp(-jnp.exp(-softplus - 0.5))
        as_ref[...] = _sigmoid(a0_ref[d:d + 1, :] + la[:, cols])


def rwkv_prep(x, mod, rp, seq_len, *, tm=256):
    t_rows = x.shape[0]
    row = pl.BlockSpec((tm, D_MODEL), lambda i: (i, 0))
    halo = tm // 8
    last8 = t_rows // 8 - 1
    full = lambda a: pl.BlockSpec(a.shape, lambda i: (0,) * a.ndim)
    weights = [rp["mu"], rp["w_rkv"], rp["w1"], rp["w2"], rp["a1"], rp["a2"], rp["g1"], rp["g2"], rp["w0"], rp["a0"]]
    return pl.pallas_call(
        functools.partial(_rwkv_prep_kernel, seq_len=seq_len),
        grid=(t_rows // tm,),
        in_specs=[row,
                  pl.BlockSpec((8, D_MODEL), lambda i: (jnp.maximum(i * halo - 1, 0), 0)),
                  pl.BlockSpec((8, D_MODEL), lambda i: (jnp.minimum((i + 1) * halo, last8), 0)),
                  _mod_spec(1, tm, seq_len), _mod_spec(0, tm, seq_len)] + [full(a) for a in weights],
        out_specs=[row] * 8,
        out_shape=[jax.ShapeDtypeStruct((t_rows, D_MODEL), F32)] * 8,
        compiler_params=_params("parallel"),
        name="rwkv_prep",
    )(x, x, x, mod, mod, *weights)


def _wkv_scan_kernel(*refs, reverse, tt):
    if reverse:
        (r_ref, k_ref, v_ref, w_ref, as_ref, kkp_ref, kap_ref, rkp_ref, gng_ref, gnb_ref, yin_ref, bin_ref,
         y_ref, s_ref, na_ref, b_ref, kd_ref) = refs
    else:
        (r_ref, k_ref, v_ref, w_ref, as_ref, kkp_ref, kap_ref, rkp_ref,
         y_ref, bout_ref, s_ref, na_ref, b_ref, kd_ref) = refs
    n = RWKV_HEAD

    @pl.when(pl.program_id(1) == 0)
    def _():
        s_ref[...] = jnp.zeros_like(s_ref)

    r = r_ref[...]
    k = k_ref[...]
    a_sig = as_ref[...]
    kk = k * kkp_ref[...][None]
    kk = kk / jnp.maximum(jnp.sqrt(jnp.sum(kk * kk, 1, keepdims=True)), 1e-12)
    kd = k * (1.0 + (a_sig - 1.0) * kap_ref[...][None])
    na_ref[...] = -kk
    b_ref[...] = kk * a_sig
    kd_ref[...] = kd
    bonus = jnp.sum(r * kd * rkp_ref[...][None], 1, keepdims=True)

    def time_index(j):
        return tt - 1 - j if reverse else j

    a_first = na_ref[time_index(0)]
    sa0 = jnp.zeros((n, LANES), F32)
    for kc in range(n):
        sa0 = sa0 + s_ref[kc] * a_first[kc:kc + 1, :]

    def step(j, sa):
        t = time_index(j)
        a_next = na_ref[time_index(jnp.minimum(j + 1, tt - 1))]
        w_t = w_ref[t]
        b_t = b_ref[t]
        kd_t = kd_ref[t]
        r_t = r_ref[t]
        v_t = v_ref[t]
        y = jnp.zeros((n, LANES), F32)
        sa_next = jnp.zeros((n, LANES), F32)
        for kc in range(n):
            s_new = s_ref[kc] * w_t[kc:kc + 1, :] + sa * b_t[kc:kc + 1, :] + v_t * kd_t[kc:kc + 1, :]
            s_ref[kc] = s_new
            y = y + s_new * r_t[kc:kc + 1, :]
            sa_next = sa_next + s_new * a_next[kc:kc + 1, :]
        y_ref[t] = y
        return sa_next

    lax.fori_loop(0, tt, step, sa0)

    if reverse:
        y = y_ref[...] + yin_ref[...]
        mean = jnp.mean(y, 1, keepdims=True)
        yc = y - mean
        var = jnp.mean(yc * yc, 1, keepdims=True)
        yn = yc * lax.rsqrt(var + RWKV_GN_EPS) * gng_ref[...][None] + gnb_ref[...][None]
        y_ref[...] = yn + (bonus + bin_ref[...]) * v_ref[...]
    else:
        bout_ref[...] = bonus


def wkv_scan(r, k, v, w, a_sig, chan, *, reverse, y_in=None, bonus_in=None, tt=32):
    seq_len, n, chains = r.shape
    nt = seq_len // tt
    tmap = (lambda c, j: (nt - 1 - j, 0, c)) if reverse else (lambda c, j: (j, 0, c))
    blk = pl.BlockSpec((tt, n, LANES), tmap)
    bblk = pl.BlockSpec((tt, 1, LANES), tmap)
    par = pl.BlockSpec((n, LANES), lambda c, j: (0, 0))
    in_specs = [blk] * 5 + [par] * 3
    args = [r, k, v, w, a_sig, chan["k_k"], chan["k_a"], chan["r_k"]]
    if reverse:
        in_specs += [par, par, blk, bblk]
        args += [chan["gn_g"], chan["gn_b"], y_in, bonus_in]
        out_specs = blk
        out_shape = jax.ShapeDtypeStruct(r.shape, F32)
    else:
        out_specs = [blk, bblk]
        out_shape = [jax.ShapeDtypeStruct(r.shape, F32), jax.ShapeDtypeStruct((seq_len, 1, chains), F32)]
    return pl.pallas_call(
        functools.partial(_wkv_scan_kernel, reverse=reverse, tt=tt),
        grid=(chains // LANES, nt),
        in_specs=in_specs,
        out_specs=out_specs,
        out_shape=out_shape,
        scratch_shapes=[pltpu.VMEM((n, n, LANES), F32)] + [pltpu.VMEM((tt, n, LANES), F32)] * 3,
        compiler_params=_params("parallel", "arbitrary"),
        name="wkv_scan_rev" if reverse else "wkv_scan_fwd",
    )(*args)


def rwkv_mixer(x, mod, rp, nb, seq_len):
    r, k, v, dec0, dec1, as0, as1, g = rwkv_prep(x, mod, rp, seq_len)
    chains = nb * RWKV_HEADS

    def to_chains(a):
        return a.reshape(nb, seq_len, RWKV_HEADS, RWKV_HEAD).transpose(1, 3, 0, 2).reshape(seq_len, RWKV_HEAD, chains)

    rt, kt, vt = to_chains(r), to_chains(k), to_chains(v)
    y_f, bonus_f = wkv_scan(rt, kt, vt, to_chains(dec0), to_chains(as0), rp["chan"][0], reverse=False)
    y = wkv_scan(rt, kt, vt, to_chains(dec1), to_chains(as1), rp["chan"][1], reverse=True,
                 y_in=y_f, bonus_in=bonus_f)
    y = y.reshape(seq_len, RWKV_HEAD, nb, RWKV_HEADS).transpose(2, 0, 3, 1).reshape(nb * seq_len, D_MODEL)
    return y, g


def _diff_attn_kernel(lam_ref, g_ref, q_ref, k_ref, v_ref, o_ref, *, lam_init):
    lf = lam_ref[...]
    lam = (jnp.exp(jnp.sum(lf[0:1] * lf[1:2], -1, keepdims=True))
           - jnp.exp(jnp.sum(lf[2:3] * lf[3:4], -1, keepdims=True)) + lam_init)
    q = q_ref[...]
    k = k_ref[...]

    def softmax_terms(cols):
        s = lax.dot_general(q[:, cols], k[:, cols], (((1,), (1,)), ((), ())), preferred_element_type=F32)
        e = jnp.exp2(s - jnp.max(s, -1, keepdims=True))
        return e, 1.0 / jnp.sum(e, -1, keepdims=True)

    e1, inv1 = softmax_terms(slice(0, DA_HEAD))
    e2, inv2 = softmax_terms(slice(DA_HEAD, 2 * DA_HEAD))
    a = e1 * inv1 - e2 * (lam * inv2)
    o = _dot(a.astype(BF16), v_ref[...])
    o = o * lax.rsqrt(jnp.mean(o * o, -1, keepdims=True) + SUBLN_EPS) * g_ref[...] * (1.0 - lam_init)
    o_ref[...] = o.astype(o_ref.dtype)


def diff_attention(qkv, lam, subln_g, layer_idx, nb, seq_len, *, tq=256):
    lam_init = 0.8 - 0.6 * math.exp(-0.3 * layer_idx)
    hd = 2 * DA_HEAD
    nq = seq_len // tq
    return pl.pallas_call(
        functools.partial(_diff_attn_kernel, lam_init=lam_init),
        grid=(nb, DA_HEADS, nq),
        in_specs=[pl.BlockSpec((4, DA_HEAD), lambda b, h, i: (0, 0)),
                  pl.BlockSpec((1, hd), lambda b, h, i: (0, 0)),
                  pl.BlockSpec((tq, hd), lambda b, h, i: (b * nq + i, h)),
                  pl.BlockSpec((seq_len, hd), lambda b, h, i: (b, DA_HEADS + h)),
                  pl.BlockSpec((seq_len, hd), lambda b, h, i: (b, 2 * DA_HEADS + h))],
        out_specs=pl.BlockSpec((tq, hd), lambda b, h, i: (b * nq + i, h)),
        out_shape=jax.ShapeDtypeStruct((nb * seq_len, D_MODEL), BF16),
        compiler_params=_params("parallel", "parallel", "arbitrary"),
        name="diff_attention",
    )(lam, subln_g.reshape(1, hd), qkv, qkv, qkv)


def _chan_table(p):
    t = p.reshape(RWKV_HEADS, RWKV_HEAD).T
    return jnp.tile(t, (1, LANES // RWKV_HEADS))


def _prepare_weights(P):
    bf = lambda a: a.astype(BF16)
    W = {"hy": [], "rw": [], "da": []}
    for j in range(P["hy_w_in"].shape[0]):
        W["hy"].append(dict(
            w_in=bf(P["hy_w_in"][j]), conv_w=P["hy_conv_w"][j], conv_b=P["hy_conv_b"][j],
            f_w1p=jnp.pad(P["hy_f_w1"][j], ((0, LANES - P["hy_f_w1"].shape[1]), (0, 0))),
            f_b1=P["hy_f_b1"][j], f_w2=P["hy_f_w2"][j], f_b2=P["hy_f_b2"][j], f_w3=P["hy_f_w3"][j],
            f_b3=P["hy_f_b3"][j], f_w4=P["hy_f_w4"][j], f_freq=P["hy_f_freq"][j], skip=P["hy_bias"][j],
            w_out=bf(P["hy_w_out"][j])))
    for j in range(P["rw_w_rkv"].shape[0]):
        zero = jnp.zeros((RWKV_LORA, D_MODEL), F32)
        blockdiag = lambda m: jnp.concatenate([jnp.concatenate([m[0], zero], 1), jnp.concatenate([zero, m[1]], 1)], 0)
        gpad = 2 * LANES - GATE_LORA
        W["rw"].append(dict(
            mu=P["rw_mu"][j], w_rkv=bf(P["rw_w_rkv"][j]),
            w1=bf(jnp.concatenate([P["rw_w1"][j, 0], P["rw_w1"][j, 1]], 1)), w2=bf(blockdiag(P["rw_w2"][j])),
            a1=bf(jnp.concatenate([P["rw_a1"][j, 0], P["rw_a1"][j, 1]], 1)), a2=bf(blockdiag(P["rw_a2"][j])),
            g1=bf(jnp.pad(P["rw_g1"][j], ((0, 0), (0, gpad)))), g2=bf(jnp.pad(P["rw_g2"][j], ((0, gpad), (0, 0)))),
            w0=P["rw_w0"][j], a0=P["rw_a0"][j],
            chan=[dict(k_k=_chan_table(P["rw_k_k"][j]), k_a=_chan_table(P["rw_k_a"][j]),
                       r_k=_chan_table(P["rw_r_k"][j, d]), gn_g=_chan_table(P["rw_gn_g"][j]),
                       gn_b=_chan_table(P["rw_gn_b"][j])) for d in range(2)],
            w_out=bf(P["rw_w_out"][j])))
    for j in range(P["da_w_qkv"].shape[0]):
        W["da"].append(dict(w_qkv=bf(P["da_w_qkv"][j]), lam=P["da_lam"][j], subln_g=P["da_subln_g"][j],
                            w_out=bf(P["da_w_out"][j])))
    W["ff_w_gu"], W["ff_w_down"] = bf(P["ff_w_gu"]), bf(P["ff_w_down"])
    W["moe_w_gu"], W["moe_w_down"] = bf(P["moe_w_gu"]), bf(P["moe_w_down"])
    return W


def _trunk(x, mods, P, W):
    nb, seq_len, _ = x.shape
    x = x.reshape(nb * seq_len, D_MODEL)
    for i in range(DEPTH):
        mod = mods[i].reshape(nb * 6, 1, D_MODEL)
        kind, j = i % N_MIXERS, i // N_MIXERS
        zgate = None
        if kind == 0:
            z = hyena_mixer(x, mod, W["hy"][j], nb, seq_len)
            w_out = W["hy"][j]["w_out"]
        elif kind == 1:
            z, zgate = rwkv_mixer(x, mod, W["rw"][j], nb, seq_len)
            w_out = W["rw"][j]["w_out"]
        else:
            da = W["da"][j]
            qkv = qkv_rope_projection(x, mod, da["w_qkv"], seq_len)
            z = diff_attention(qkv, da["lam"], da["subln_g"], i, nb, seq_len)
            w_out = da["w_out"]
        x = out_projection_ln(z, x, mod, w_out, P["ln_g"][i, 0], P["ln_b"][i, 0], seq_len, zgate=zgate)
        if i % 2 == 0:
            x = swiglu_ln(x, mod, W["ff_w_gu"][i // 2], W["ff_w_down"][i // 2],
                          P["ln_g"][i, 1], P["ln_b"][i, 1], seq_len)
        else:
            x = moe_ln(x, mod, P["moe_router"][i // 2], W["moe_w_gu"][i // 2], W["moe_w_down"][i // 2],
                       P["ln_g"][i, 1], P["ln_b"][i, 1], seq_len)
    return x.reshape(nb, seq_len, D_MODEL)


def kernel(x_prompt, x_sample, c_prompt, c_sample, ada_w, ada_b, ln_g, ln_b, hy_w_in, hy_conv_w, hy_conv_b, hy_f_w1, hy_f_b1, hy_f_w2, hy_f_b2, hy_f_w3, hy_f_b3, hy_f_w4, hy_f_freq, hy_bias, hy_w_out, rw_mu, rw_w_rkv, rw_w0, rw_w1, rw_w2, rw_a0, rw_a1, rw_a2, rw_g1, rw_g2, rw_k_k, rw_k_a, rw_r_k, rw_gn_g, rw_gn_b, rw_w_out, da_w_qkv, da_lam, da_subln_g, da_w_out, ff_w_gu, ff_w_down, moe_router, moe_w_gu, moe_w_down):
    P = dict(ada_w=ada_w, ada_b=ada_b, ln_g=ln_g, ln_b=ln_b,
             hy_w_in=hy_w_in, hy_conv_w=hy_conv_w, hy_conv_b=hy_conv_b,
             hy_f_w1=hy_f_w1, hy_f_b1=hy_f_b1, hy_f_w2=hy_f_w2, hy_f_b2=hy_f_b2,
             hy_f_w3=hy_f_w3, hy_f_b3=hy_f_b3, hy_f_w4=hy_f_w4, hy_f_freq=hy_f_freq,
             hy_bias=hy_bias, hy_w_out=hy_w_out,
             rw_mu=rw_mu, rw_w_rkv=rw_w_rkv, rw_w0=rw_w0, rw_w1=rw_w1, rw_w2=rw_w2,
             rw_a0=rw_a0, rw_a1=rw_a1, rw_a2=rw_a2, rw_g1=rw_g1, rw_g2=rw_g2,
             rw_k_k=rw_k_k, rw_k_a=rw_k_a, rw_r_k=rw_r_k, rw_gn_g=rw_gn_g, rw_gn_b=rw_gn_b,
             rw_w_out=rw_w_out,
             da_w_qkv=da_w_qkv, da_lam=da_lam, da_subln_g=da_subln_g, da_w_out=da_w_out,
             ff_w_gu=ff_w_gu, ff_w_down=ff_w_down,
             moe_router=moe_router, moe_w_gu=moe_w_gu, moe_w_down=moe_w_down)
    W = _prepare_weights(P)
    n_prompt = x_prompt.shape[0]
    mods = ada_modulation(jnp.concatenate([c_prompt, c_sample], 0), ada_w, ada_b)
    y_prompt = _trunk(x_prompt, mods[:, :n_prompt], P, W)
    y_sample = _trunk(x_sample, mods[:, n_prompt:], P, W)
    return (y_prompt, y_sample)
```

```python
import functools
import math

import jax
import jax.numpy as jnp
from jax import lax
from jax.experimental import pallas as pl
from jax.experimental.pallas import tpu as pltpu

F32 = jnp.float32
BF16 = jnp.bfloat16

D_MODEL = 1024
DEPTH = 4
N_MIXERS = 3
DEEPNORM_ALPHA = (2 * DEPTH) ** 0.25
LN_EPS = 1e-5

HYENA_ORDER = 2
FILTER_BANDS = 16
FILTER_HIDDEN = 64
HYENA_MIN_DECAY = math.log(1e-2) / 1.5
HYENA_MAX_DECAY = math.log(1e-2) / 0.3
FILTER_NORM_EPS = 1e-6

RWKV_HEAD = 64
RWKV_HEADS = D_MODEL // RWKV_HEAD
RWKV_LORA = 64
GATE_LORA = 160
RWKV_GN_EPS = 64e-5

DA_HEAD = 64
DA_HEADS = D_MODEL // (2 * DA_HEAD)
ROPE_THETA = 10000.0
SUBLN_EPS = 1e-5

D_FF = 2816
N_EXPERTS = 8
D_FF_EXPERT = 3584

LANES = 128
VMEM_LIMIT_BYTES = 56 * 1024 * 1024


def _params(*semantics):
    return pltpu.CompilerParams(dimension_semantics=semantics, vmem_limit_bytes=VMEM_LIMIT_BYTES)


def _split_bf16(x):
    hi = x.astype(BF16)
    lo = (x - hi.astype(F32)).astype(BF16)
    return hi, lo


def _dot(a, b):
    return jnp.dot(a, b, preferred_element_type=F32)


def _dot_split(a, b):
    ah, al = _split_bf16(a)
    bh, bl = _split_bf16(b)
    return _dot(ah, bh) + _dot(al, bh) + _dot(ah, bl)


def _sigmoid(x):
    return 1.0 / (1.0 + jnp.exp(-x))


def _layer_norm(x, g, b):
    mu = jnp.mean(x, -1, keepdims=True)
    xc = x - mu
    var = jnp.mean(xc * xc, -1, keepdims=True)
    return xc * lax.rsqrt(var + LN_EPS) * g + b


def _mod_spec(which, tm, seq_len):
    return pl.BlockSpec((1, 1, D_MODEL), lambda i, *_: ((i * tm // seq_len) * 6 + which, 0, 0))


def _ada_kernel(c_ref, w_ref, b_ref, o_ref):
    c = c_ref[...]
    cs = (c * _sigmoid(c)).astype(BF16)
    o_ref[...] = _dot(cs, w_ref[...].astype(BF16)) + b_ref[...]


def ada_modulation(c, ada_w, ada_b):
    nb = c.shape[0]
    tn = 1024
    return pl.pallas_call(
        _ada_kernel,
        grid=(DEPTH, 6 * D_MODEL // tn),
        in_specs=[pl.BlockSpec((nb, D_MODEL), lambda l, j: (0, 0)),
                  pl.BlockSpec((None, D_MODEL, tn), lambda l, j: (l, 0, j)),
                  pl.BlockSpec((None, 1, tn), lambda l, j: (l, 0, j))],
        out_specs=pl.BlockSpec((None, nb, tn), lambda l, j: (l, 0, j)),
        out_shape=jax.ShapeDtypeStruct((DEPTH, nb, 6 * D_MODEL), F32),
        compiler_params=_params("parallel", "parallel"),
        name="ada_modulation",
    )(c, ada_w, ada_b.reshape(DEPTH, 1, 6 * D_MODEL))


def _proj_kernel(x_ref, sc_ref, sh_ref, w_ref, o_ref, xb_ref):
    @pl.when(pl.program_id(1) == 0)
    def _():
        xb_ref[...] = (x_ref[...] * (1.0 + sc_ref[0]) + sh_ref[0]).astype(BF16)

    o_ref[...] = _dot(xb_ref[...], w_ref[...]).astype(o_ref.dtype)


def _qkv_rope_kernel(x_ref, sc_ref, sh_ref, w_ref, cos_ref, sin_ref, o_ref, xb_ref, *, n_rope_tiles):
    j = pl.program_id(1)

    @pl.when(j == 0)
    def _():
        xb_ref[...] = (x_ref[...] * (1.0 + sc_ref[0]) + sh_ref[0]).astype(BF16)

    y = _dot(xb_ref[...], w_ref[...])
    tn = y.shape[1]

    @pl.when(j < n_rope_tiles)
    def _():
        lane = lax.broadcasted_iota(jnp.int32, y.shape, 1)
        first_half = (lane % DA_HEAD) < (DA_HEAD // 2)
        partner = jnp.where(first_half,
                            pltpu.roll(y, tn - DA_HEAD // 2, axis=1),
                            pltpu.roll(y, DA_HEAD // 2, axis=1))
        o_ref[...] = (y * cos_ref[...] + partner * sin_ref[...]).astype(o_ref.dtype)

    @pl.when(j >= n_rope_tiles)
    def _():
        o_ref[...] = y.astype(o_ref.dtype)


def modulated_projection(x, mod, w, seq_len, *, tm=1024, tn=1024):
    t_rows, n_out = x.shape[0], w.shape[1]
    return pl.pallas_call(
        _proj_kernel,
        grid=(t_rows // tm, n_out // tn),
        in_specs=[pl.BlockSpec((tm, D_MODEL), lambda i, j: (i, 0)),
                  _mod_spec(1, tm, seq_len), _mod_spec(0, tm, seq_len),
                  pl.BlockSpec((D_MODEL, tn), lambda i, j: (0, j))],
        out_specs=pl.BlockSpec((tm, tn), lambda i, j: (i, j)),
        out_shape=jax.ShapeDtypeStruct((t_rows, n_out), F32),
        scratch_shapes=[pltpu.VMEM((tm, D_MODEL), BF16)],
        compiler_params=_params("parallel", "arbitrary"),
        name="modulated_projection",
    )(x, mod, mod, w)


def qkv_rope_projection(x, mod, w, seq_len, *, tm=1024, tn=512):
    t_rows = x.shape[0]
    tiles_per_seq = seq_len // tm
    nq_tiles = D_MODEL // tn
    half = DA_HEAD // 2
    inv = ROPE_THETA ** (-jnp.arange(0, DA_HEAD, 2, dtype=F32) / DA_HEAD)
    ang = jnp.arange(seq_len, dtype=F32)[:, None] * inv[None, :]
    cos_t = jnp.tile(jnp.concatenate([jnp.cos(ang), jnp.cos(ang)], -1), (1, tn // DA_HEAD))
    sin_t = jnp.tile(jnp.concatenate([-jnp.sin(ang), jnp.sin(ang)], -1), (1, tn // DA_HEAD))
    qscale = jnp.float32(DA_HEAD ** -0.5 * math.log2(math.e))
    cos_all = jnp.concatenate([cos_t * qscale, cos_t], 0)
    sin_all = jnp.concatenate([sin_t * qscale, sin_t], 0)
    tab = lambda i, j: (jnp.where(j < nq_tiles, 0, tiles_per_seq) + i % tiles_per_seq, 0)
    return pl.pallas_call(
        functools.partial(_qkv_rope_kernel, n_rope_tiles=2 * nq_tiles),
        grid=(t_rows // tm, 3 * D_MODEL // tn),
        in_specs=[pl.BlockSpec((tm, D_MODEL), lambda i, j: (i, 0)),
                  _mod_spec(1, tm, seq_len), _mod_spec(0, tm, seq_len),
                  pl.BlockSpec((D_MODEL, tn), lambda i, j: (0, j)),
                  pl.BlockSpec((tm, tn), tab), pl.BlockSpec((tm, tn), tab)],
        out_specs=pl.BlockSpec((tm, tn), lambda i, j: (i, j)),
        out_shape=jax.ShapeDtypeStruct((t_rows, 3 * D_MODEL), BF16),
        scratch_shapes=[pltpu.VMEM((tm, D_MODEL), BF16)],
        compiler_params=_params("parallel", "arbitrary"),
        name="qkv_rope_projection",
    )(x, mod, mod, w, cos_all, sin_all)


def _out_ln_kernel(*refs, has_zgate):
    if has_zgate:
        z_ref, zg_ref, x_ref, gt_ref, w_ref, g_ref, b_ref, o_ref = refs
        z = (z_ref[...].astype(F32) * zg_ref[...]).astype(BF16)
    else:
        z_ref, x_ref, gt_ref, w_ref, g_ref, b_ref, o_ref = refs
        z = z_ref[...].astype(BF16)
    y = _dot(z, w_ref[...])
    o_ref[...] = _layer_norm(DEEPNORM_ALPHA * x_ref[...] + (1.0 + gt_ref[0]) * y, g_ref[...], b_ref[...])


def out_projection_ln(z, x, mod, w, ln_g, ln_b, seq_len, *, zgate=None, tm=512):
    t_rows = x.shape[0]
    row = pl.BlockSpec((tm, D_MODEL), lambda i: (i, 0))
    vec = pl.BlockSpec((1, D_MODEL), lambda i: (0, 0))
    in_specs = [row] + ([row] if zgate is not None else []) + [
        row, _mod_spec(2, tm, seq_len), pl.BlockSpec((D_MODEL, D_MODEL), lambda i: (0, 0)), vec, vec]
    args = [z] + ([zgate] if zgate is not None else []) + [
        x, mod, w, ln_g.reshape(1, D_MODEL), ln_b.reshape(1, D_MODEL)]
    return pl.pallas_call(
        functools.partial(_out_ln_kernel, has_zgate=zgate is not None),
        grid=(t_rows // tm,),
        in_specs=in_specs,
        out_specs=row,
        out_shape=jax.ShapeDtypeStruct((t_rows, D_MODEL), F32),
        compiler_params=_params("parallel"),
        name="out_projection_ln",
    )(*args)


def _ffn_kernel(x_ref, sc_ref, sh_ref, gt_ref, wg_ref, wu_ref, wd_ref, g_ref, b_ref, o_ref, hb_ref, acc_ref):
    f = pl.program_id(1)

    @pl.when(f == 0)
    def _():
        hb_ref[...] = (x_ref[...] * (1.0 + sc_ref[0]) + sh_ref[0]).astype(BF16)
        acc_ref[...] = jnp.zeros_like(acc_ref)

    hb = hb_ref[...]
    gate = _dot(hb, wg_ref[...])
    up = _dot(hb, wu_ref[...])
    act = (gate * _sigmoid(gate) * up).astype(BF16)
    acc_ref[...] += _dot(act, wd_ref[...])

    @pl.when(f == pl.num_programs(1) - 1)
    def _():
        o_ref[...] = _layer_norm(DEEPNORM_ALPHA * x_ref[...] + (1.0 + gt_ref[0]) * acc_ref[...],
                                 g_ref[...], b_ref[...])


def swiglu_ln(x, mod, w_gu, w_down, ln_g, ln_b, seq_len, *, tm=512, tf=1408):
    t_rows = x.shape[0]
    nf = D_FF // tf
    row = pl.BlockSpec((tm, D_MODEL), lambda i, f: (i, 0))
    vec = pl.BlockSpec((1, D_MODEL), lambda i, f: (0, 0))
    return pl.pallas_call(
        _ffn_kernel,
        grid=(t_rows // tm, nf),
        in_specs=[row, _mod_spec(4, tm, seq_len), _mod_spec(3, tm, seq_len), _mod_spec(5, tm, seq_len),
                  pl.BlockSpec((D_MODEL, tf), lambda i, f: (0, f)),
                  pl.BlockSpec((D_MODEL, tf), lambda i, f: (0, nf + f)),
                  pl.BlockSpec((tf, D_MODEL), lambda i, f: (f, 0)),
                  vec, vec],
        out_specs=row,
        out_shape=jax.ShapeDtypeStruct((t_rows, D_MODEL), F32),
        scratch_shapes=[pltpu.VMEM((tm, D_MODEL), BF16), pltpu.VMEM((tm, D_MODEL), F32)],
        compiler_params=_params("parallel", "arbitrary"),
        name="swiglu_ln",
    )(x, mod, mod, mod, w_gu, w_gu, w_down, ln_g.reshape(1, D_MODEL), ln_b.reshape(1, D_MODEL))


MOE_SUB = 144


def _moe_kernel(x_ref, sc_ref, sh_ref, gt_ref, r_ref, wg_ref, wu_ref, wd_ref, g_ref, b_ref, o_ref,
                hb_ref, comb_ref, slot_ref, slot_t_ref, cnt_ref, xg_ref, yg_ref, acc_ref):
    e = pl.program_id(1)
    f = pl.program_id(2)
    tm = x_ref.shape[0]
    sub = MOE_SUB

    @pl.when((e == 0) & (f == 0))
    def _():
        h = x_ref[...] * (1.0 + sc_ref[0]) + sh_ref[0]
        hb_ref[...] = h.astype(BF16)
        acc_ref[...] = jnp.zeros_like(acc_ref)
        logits = _dot_split(h, r_ref[...])
        lane = lax.broadcasted_iota(jnp.int32, logits.shape, 1)
        neg = jnp.float32(-jnp.inf)
        lg = jnp.where(lane < N_EXPERTS, logits, neg)
        m1 = jnp.max(lg, -1, keepdims=True)
        i1 = jnp.min(jnp.where(lg == m1, lane, LANES), -1, keepdims=True)
        lg2 = jnp.where(lane == i1, neg, lg)
        m2 = jnp.max(lg2, -1, keepdims=True)
        i2 = jnp.min(jnp.where(lg2 == m2, lane, LANES), -1, keepdims=True)
        e2 = jnp.exp(m2 - m1)
        g1 = 1.0 / (1.0 + e2)
        comb = jnp.where(lane == i1, g1, 0.0) + jnp.where(lane == i2, e2 * g1, 0.0)
        comb_ref[...] = comb
        sel = comb > 0.0
        selb = jnp.where(sel, 1.0, 0.0).astype(BF16)
        chunk = 256
        for r0 in range(0, tm, chunk):
            rr = lax.broadcasted_iota(jnp.int32, (chunk, tm), 0) + r0
            cc = lax.broadcasted_iota(jnp.int32, (chunk, tm), 1)
            before = jnp.where(cc < rr, 1.0, 0.0).astype(BF16)
            slot_ref[r0:r0 + chunk, :] = _dot(before, selb)
        slot = jnp.where(sel, slot_ref[...], -1.0)
        slot_ref[...] = slot
        slot_t_ref[...] = slot.T
        cnt_ref[...] = jnp.sum(jnp.where(sel, 1.0, 0.0), 0, keepdims=True)

    lane1 = lax.broadcasted_iota(jnp.int32, cnt_ref.shape, 1)
    n_e = jnp.sum(jnp.where(lane1 == e, cnt_ref[...], 0.0)).astype(jnp.int32)
    n_sub = (n_e + sub - 1) // sub

    def for_sub_blocks(fn):
        def pair(s, carry):
            fn(pl.multiple_of(s * (2 * sub), 2 * sub), 2 * sub)
            return carry

        lax.fori_loop(0, n_sub // 2, pair, 0)

        @pl.when(n_sub % 2 == 1)
        def _():
            fn(pl.multiple_of((n_sub - 1) * sub, sub), sub)

    @pl.when(f == 0)
    def _():
        slot_row = slot_t_ref[pl.ds(e, 1), :]

        def gather(base, rows):
            want = (lax.broadcasted_iota(jnp.int32, (rows, tm), 0) + base).astype(F32)
            pick = jnp.where(slot_row == want, 1.0, 0.0).astype(BF16)
            xg_ref[pl.ds(base, rows), :] = _dot(pick, hb_ref[...]).astype(BF16)
            yg_ref[pl.ds(base, rows), :] = jnp.zeros((rows, D_MODEL), F32)

        for_sub_blocks(gather)

    def expert_ffn(base, rows):
        xs = xg_ref[pl.ds(base, rows), :]
        gate = _dot(xs, wg_ref[...])
        up = _dot(xs, wu_ref[...])
        act = (gate * _sigmoid(gate) * up).astype(BF16)
        yg_ref[pl.ds(base, rows), :] += _dot(act, wd_ref[...])

    for_sub_blocks(expert_ffn)

    @pl.when(f == pl.num_programs(2) - 1)
    def _():
        lane = lax.broadcasted_iota(jnp.int32, comb_ref.shape, 1)
        mine = lane == e
        slot_col = jnp.sum(jnp.where(mine, slot_ref[...], 0.0), -1, keepdims=True)
        c_e = jnp.sum(jnp.where(mine, comb_ref[...], 0.0), -1, keepdims=True)

        def scatter(base, rows):
            have = (lax.broadcasted_iota(jnp.int32, (tm, rows), 1) + base).astype(F32)
            put = jnp.where(slot_col == have, 1.0, 0.0).astype(BF16)
            acc_ref[...] += c_e * _dot(put, yg_ref[pl.ds(base, rows), :].astype(BF16))

        for_sub_blocks(scatter)

    @pl.when((e == pl.num_programs(1) - 1) & (f == pl.num_programs(2) - 1))
    def _():
        o_ref[...] = _layer_norm(DEEPNORM_ALPHA * x_ref[...] + (1.0 + gt_ref[0]) * acc_ref[...],
                                 g_ref[...], b_ref[...])


def moe_ln(x, mod, router, w_gu, w_down, ln_g, ln_b, seq_len, *, tm=1024, tf=896):
    t_rows = x.shape[0]
    nf = D_FF_EXPERT // tf
    router_p = jnp.pad(router, ((0, 0), (0, LANES - N_EXPERTS)))
    slots = -(-tm // MOE_SUB) * MOE_SUB
    row = pl.BlockSpec((tm, D_MODEL), lambda i, e, f: (i, 0))
    vec = pl.BlockSpec((1, D_MODEL), lambda i, e, f: (0, 0))
    return pl.pallas_call(
        _moe_kernel,
        grid=(t_rows // tm, N_EXPERTS, nf),
        in_specs=[row, _mod_spec(4, tm, seq_len), _mod_spec(3, tm, seq_len), _mod_spec(5, tm, seq_len),
                  pl.BlockSpec((D_MODEL, LANES), lambda i, e, f: (0, 0)),
                  pl.BlockSpec((None, D_MODEL, tf), lambda i, e, f: (e, 0, f)),
                  pl.BlockSpec((None, D_MODEL, tf), lambda i, e, f: (e, 0, nf + f)),
                  pl.BlockSpec((None, tf, D_MODEL), lambda i, e, f: (e, f, 0)),
                  vec, vec],
        out_specs=row,
        out_shape=jax.ShapeDtypeStruct((t_rows, D_MODEL), F32),
        scratch_shapes=[pltpu.VMEM((tm, D_MODEL), BF16),
                        pltpu.VMEM((tm, LANES), F32),
                        pltpu.VMEM((tm, LANES), F32),
                        pltpu.VMEM((LANES, tm), F32),
                        pltpu.VMEM((1, LANES), F32),
                        pltpu.VMEM((slots, D_MODEL), BF16),
                        pltpu.VMEM((slots, D_MODEL), F32),
                        pltpu.VMEM((tm, D_MODEL), F32)],
        compiler_params=_params("parallel", "arbitrary", "arbitrary"),
        name="moe_ln",
    )(x, mod, mod, mod, router_p, w_gu, w_gu, w_down, ln_g.reshape(1, D_MODEL), ln_b.reshape(1, D_MODEL))


def _short_conv_kernel(u_ref, w_ref, b_ref, o_ref):
    u = u_ref[...]
    n = u.shape[0]
    row = lax.broadcasted_iota(jnp.int32, u.shape, 0)
    prev = jnp.where(row == 0, 0.0, pltpu.roll(u, 1, axis=0))
    nxt = jnp.where(row == n - 1, 0.0, pltpu.roll(u, n - 1, axis=0))
    o_ref[...] = prev * w_ref[0:1, :] + u * w_ref[1:2, :] + nxt * w_ref[2:3, :] + b_ref[...]


def short_conv(u, conv_w, conv_b, *, tn=256):
    nb, seq_len, width = u.shape
    blk = pl.BlockSpec((None, seq_len, tn), lambda b, j: (b, 0, j))
    return pl.pallas_call(
        _short_conv_kernel,
        grid=(nb, width // tn),
        in_specs=[blk, pl.BlockSpec((3, tn), lambda b, j: (0, j)), pl.BlockSpec((1, tn), lambda b, j: (0, j))],
        out_specs=blk,
        out_shape=jax.ShapeDtypeStruct(u.shape, F32),
        compiler_params=_params("parallel", "parallel"),
        name="short_conv",
    )(u, conv_w, conv_b.reshape(1, width))


def _filter_kernel(z_ref, w1_ref, b1_ref, w2_ref, b2_ref, w3_ref, b3_ref, w4_ref, fr_ref, dl_ref,
                   sum_ref, dif_ref, nrm_ref, nyq_ref):
    i = pl.program_id(0)
    z = z_ref[...]
    fr = fr_ref[...]
    h = jnp.sin(fr * (_dot_split(z, w1_ref[...]) + b1_ref[...]))
    h = jnp.sin(fr * (_dot_split(h, w2_ref[...]) + b2_ref[...]))
    h = jnp.sin(fr * (_dot_split(h, w3_ref[...]) + b3_ref[...]))
    hf = _dot_split(h, w4_ref[...])
    win = jnp.exp(-z[:, 0:1] * dl_ref[...])
    win = jnp.concatenate([win] * HYENA_ORDER, axis=1)
    half = HYENA_ORDER * D_MODEL
    row = lax.broadcasted_iota(jnp.int32, (z.shape[0], half), 0) + i * z.shape[0]
    kf = hf[:, :half] * win
    kb = jnp.where(row == 0, 0.0, hf[:, half:] * win)
    ksum = kf + kb
    sum_ref[...] = ksum.astype(sum_ref.dtype)
    dif_ref[...] = (kf - kb).astype(dif_ref.dtype)
    sign = (1 - 2 * (row % 2)).astype(F32)

    @pl.when(i == 0)
    def _():
        nrm_ref[...] = jnp.zeros_like(nrm_ref)
        nyq_ref[...] = jnp.zeros_like(nyq_ref)

    nrm_ref[...] += jnp.sum(jnp.abs(kf) + jnp.abs(kb), 0, keepdims=True)
    nyq_ref[...] += jnp.sum(ksum * sign, 0, keepdims=True)


def hyena_filter_taps(feat, w1p, b1, w2, b2, w3, b3, w4, freq, deltas, *, tl=256):
    seq_len = feat.shape[0]
    half = HYENA_ORDER * D_MODEL
    full = lambda a: pl.BlockSpec(a.shape, lambda i: (0,) * a.ndim)
    args = [feat, w1p, b1.reshape(1, -1), w2, b2.reshape(1, -1), w3, b3.reshape(1, -1), w4,
            freq.reshape(1, -1), deltas.reshape(1, -1)]
    return pl.pallas_call(
        _filter_kernel,
        grid=(seq_len // tl,),
        in_specs=[pl.BlockSpec((tl, LANES), lambda i: (i, 0))] + [full(a) for a in args[1:]],
        out_specs=[pl.BlockSpec((tl, half), lambda i: (i, 0)), pl.BlockSpec((tl, half), lambda i: (i, 0)),
                   pl.BlockSpec((1, half), lambda i: (0, 0)), pl.BlockSpec((1, half), lambda i: (0, 0))],
        out_shape=[jax.ShapeDtypeStruct((seq_len, half), BF16), jax.ShapeDtypeStruct((seq_len, half), BF16),
                   jax.ShapeDtypeStruct((1, half), F32), jax.ShapeDtypeStruct((1, half), F32)],
        compiler_params=_params("arbitrary"),
        name="hyena_filter_taps",
    )(*args)


def _spectrum_kernel(m_ref, k_ref, nrm_ref, nyq_ref, o_ref, acc_ref, *, patch_row0):
    kk = pl.program_id(2)

    @pl.when(kk == 0)
    def _():
        acc_ref[...] = jnp.zeros_like(acc_ref)

    acc_ref[...] += _dot(m_ref[...], k_ref[...])

    @pl.when(kk == pl.num_programs(2) - 1)
    def _():
        inv = 1.0 / (nrm_ref[...] + FILTER_NORM_EPS)
        out = acc_ref[...] * inv
        if patch_row0:
            row = lax.broadcasted_iota(jnp.int32, out.shape, 0) + pl.program_id(0) * out.shape[0]
            out = jnp.where(row == 0, nyq_ref[...] * inv, out)
        o_ref[...] = out


def filter_spectrum(mat, taps, nrm, nyq, *, patch_row0, tm=512, tn=512, tk=512):
    seq_len, width = taps.shape
    return pl.pallas_call(
        functools.partial(_spectrum_kernel, patch_row0=patch_row0),
        grid=(seq_len // tm, width // tn, seq_len // tk),
        in_specs=[pl.BlockSpec((tm, tk), lambda i, j, k: (i, k)),
                  pl.BlockSpec((tk, tn), lambda i, j, k: (k, j)),
                  pl.BlockSpec((1, tn), lambda i, j, k: (0, j)),
                  pl.BlockSpec((1, tn), lambda i, j, k: (0, j))],
        out_specs=pl.BlockSpec((tm, tn), lambda i, j, k: (i, j)),
        out_shape=jax.ShapeDtypeStruct((seq_len, width), F32),
        scratch_shapes=[pltpu.VMEM((tm, tn), F32)],
        compiler_params=_params("parallel", "parallel", "arbitrary"),
        name="filter_spectrum",
    )(mat, taps, nrm, nyq)


def _long_conv_kernel(z_ref, gate_ref, hr_ref, hi_ref, skip_ref, fc_ref, fs_ref, gc_ref, gs_ref, o_ref,
                      zb_ref, acc_ref):
    f = pl.program_id(2)

    @pl.when(f == 0)
    def _():
        zb_ref[...] = z_ref[...].astype(BF16)
        acc_ref[...] = jnp.zeros_like(acc_ref)

    zb = zb_ref[...]
    xr = _dot(fc_ref[...], zb)
    xi = _dot(fs_ref[...], zb)
    hr = hr_ref[...]
    hi = hi_ref[...]
    row = lax.broadcasted_iota(jnp.int32, xr.shape, 0)
    packed = (row == 0) & (f == 0)
    pr = jnp.where(packed, xr * hr, xr * hr - xi * hi)
    pi = jnp.where(packed, xi * hi, xr * hi + xi * hr)
    acc_ref[...] += _dot(gc_ref[...], pr.astype(BF16)) + _dot(gs_ref[...], pi.astype(BF16))

    @pl.when(f == pl.num_programs(2) - 1)
    def _():
        conv = acc_ref[...] * (1.0 / z_ref.shape[0])
        o_ref[...] = gate_ref[...] * (conv + z_ref[...] * skip_ref[...])


def long_conv_gate(zsrc, z_col0, gsrc, g_col0, spec_r, spec_i, order, skip, mats, *, tn, tf):
    nb, seq_len, _ = zsrc.shape
    fc, fs, gc, gs = mats
    nct = D_MODEL // tn
    return pl.pallas_call(
        _long_conv_kernel,
        grid=(nb, nct, seq_len // tf),
        in_specs=[pl.BlockSpec((None, seq_len, tn), lambda b, c, f: (b, 0, z_col0 + c)),
                  pl.BlockSpec((None, seq_len, tn), lambda b, c, f: (b, 0, g_col0 + c)),
                  pl.BlockSpec((tf, tn), lambda b, c, f: (f, order * nct + c)),
                  pl.BlockSpec((tf, tn), lambda b, c, f: (f, order * nct + c)),
                  pl.BlockSpec((None, 1, tn), lambda b, c, f: (order, 0, c)),
                  pl.BlockSpec((tf, seq_len), lambda b, c, f: (f, 0)),
                  pl.BlockSpec((tf, seq_len), lambda b, c, f: (f, 0)),
                  pl.BlockSpec((seq_len, tf), lambda b, c, f: (0, f)),
                  pl.BlockSpec((seq_len, tf), lambda b, c, f: (0, f))],
        out_specs=pl.BlockSpec((None, seq_len, tn), lambda b, c, f: (b, 0, c)),
        out_shape=jax.ShapeDtypeStruct((nb, seq_len, D_MODEL), F32),
        scratch_shapes=[pltpu.VMEM((seq_len, tn), BF16), pltpu.VMEM((seq_len, tn), F32)],
        compiler_params=_params("parallel", "parallel", "arbitrary"),
        name="long_conv_gate",
    )(zsrc, gsrc, spec_r, spec_i, skip.reshape(HYENA_ORDER, 1, D_MODEL), fc, fs, gc, gs)


def dft_matrices(seq_len):
    idx = jnp.arange(seq_len, dtype=jnp.int32)
    m = (idx[:, None] * idx[None, :]) % (2 * seq_len)
    ang = m.astype(F32) * (math.pi / seq_len)
    cosm = jnp.cos(ang)
    sinm = -jnp.sin(ang)
    alt = (1 - 2 * (idx % 2)).astype(F32)
    fc = cosm
    fs = sinm.at[0, :].set(alt)
    gc = cosm.at[:, 0].set(0.5)
    gs = sinm.at[:, 0].set(0.5 * alt)
    return tuple(a.astype(BF16) for a in (fc, fs, gc, gs))


def filter_features(seq_len):
    t = jnp.linspace(0.0, 1.0, seq_len, dtype=F32)[:, None]
    omega = 2.0 * math.pi * jnp.arange(seq_len, dtype=F32)[:, None] / seq_len
    bands = jnp.linspace(1e-4, FILTER_BANDS - 1, FILTER_BANDS, dtype=F32)[None, :]
    z = jnp.concatenate([t, jnp.cos(bands * omega), -jnp.sin(bands * omega)], -1)
    return jnp.pad(z, ((0, 0), (0, LANES - z.shape[1])))


def hyena_mixer(x, mod, hp, nb, seq_len):
    tn = 512 if seq_len <= 2048 else 256
    u = modulated_projection(x, mod, hp["w_in"], seq_len)
    u = short_conv(u.reshape(nb, seq_len, 3 * D_MODEL), hp["conv_w"], hp["conv_b"])
    mats = dft_matrices(seq_len)
    deltas = jnp.abs(jnp.linspace(HYENA_MIN_DECAY, HYENA_MAX_DECAY, D_MODEL, dtype=F32))
    ksum, kdif, nrm, nyq = hyena_filter_taps(filter_features(seq_len), hp["f_w1p"], hp["f_b1"], hp["f_w2"],
                                             hp["f_b2"], hp["f_w3"], hp["f_b3"], hp["f_w4"], hp["f_freq"], deltas)
    spec_r = filter_spectrum(mats[0], ksum, nrm, nyq, patch_row0=False)
    spec_i = filter_spectrum(mats[1], kdif, nrm, nyq, patch_row0=True)
    nct = D_MODEL // tn
    tf = 512 if seq_len <= 2048 else 256
    z = long_conv_gate(u, 0, u, nct, spec_r, spec_i, 0, hp["skip"], mats, tn=tn, tf=tf)
    z = long_conv_gate(z, 0, u, 2 * nct, spec_r, spec_i, 1, hp["skip"], mats, tn=tn, tf=tf)
    return z.reshape(nb * seq_len, D_MODEL)


def _rwkv_prep_kernel(x_ref, xp_ref, xn_ref, sc_ref, sh_ref, mu_ref, wrkv_ref, w1_ref, w2_ref, a1_ref, a2_ref,
                      g1_ref, g2_ref, w0_ref, a0_ref,
                      r_ref, k_ref, v_ref, dec0_ref, dec1_ref, as0_ref, as1_ref, g_ref, *, seq_len):
    i = pl.program_id(0)
    tm = x_ref.shape[0]
    sc = 1.0 + sc_ref[0]
    sh = sh_ref[0]
    h = x_ref[...] * sc + sh
    row = lax.broadcasted_iota(jnp.int32, h.shape, 0)
    pos = (row + i * tm) % seq_len
    halo_prev = xp_ref[7:8, :] * sc + sh
    halo_next = xn_ref[0:1, :] * sc + sh
    prev = jnp.where(row == 0, halo_prev, pltpu.roll(h, 1, axis=0))
    prev = jnp.where(pos == 0, 0.0, prev)
    nxt = jnp.where(row == tm - 1, halo_next, pltpu.roll(h, tm - 1, axis=0))
    nxt = jnp.where(pos == seq_len - 1, 0.0, nxt)
    xx = 0.5 * (prev + nxt) - h

    def mix(j):
        return (h + xx * mu_ref[j:j + 1, :]).astype(BF16)

    r_ref[...] = _dot(mix(0), wrkv_ref[0])
    k_ref[...] = _dot(mix(1), wrkv_ref[1])
    v_ref[...] = _dot(mix(2), wrkv_ref[2])
    lw = _dot(jnp.tanh(_dot(mix(3), w1_ref[...])).astype(BF16), w2_ref[...])
    la = _dot(_dot(mix(4), a1_ref[...]).astype(BF16), a2_ref[...])
    g_ref[...] = _dot(_sigmoid(_dot(mix(5), g1_ref[...])).astype(BF16), g2_ref[...])
    for d, (dec_ref, as_ref) in enumerate(((dec0_ref, as0_ref), (dec1_ref, as1_ref))):
        cols = slice(d * D_MODEL, (d + 1) * D_MODEL)
        pre = -(w0_ref[d:d + 1, :] + lw[:, cols])
        softplus = jnp.maximum(pre, 0.0) + jnp.log(1.0 + jnp.exp(-jnp.abs(pre)))
        dec_ref[...] = jnp.exp(-jnp.exp(-softplus - 0.5))
        as_ref[...] = _sigmoid(a0_ref[d:d + 1, :] + la[:, cols])


def rwkv_prep(x, mod, rp, seq_len, *, tm=256):
    t_rows = x.shape[0]
    row = pl.BlockSpec((tm, D_MODEL), lambda i: (i, 0))
    halo = tm // 8
    last8 = t_rows // 8 - 1
    full = lambda a: pl.BlockSpec(a.shape, lambda i: (0,) * a.ndim)
    weights = [rp["mu"], rp["w_rkv"], rp["w1"], rp["w2"], rp["a1"], rp["a2"], rp["g1"], rp["g2"], rp["w0"], rp["a0"]]
    return pl.pallas_call(
        functools.partial(_rwkv_prep_kernel, seq_len=seq_len),
        grid=(t_rows // tm,),
        in_specs=[row,
                  pl.BlockSpec((8, D_MODEL), lambda i: (jnp.maximum(i * halo - 1, 0), 0)),
                  pl.BlockSpec((8, D_MODEL), lambda i: (jnp.minimum((i + 1) * halo, last8), 0)),
                  _mod_spec(1, tm, seq_len), _mod_spec(0, tm, seq_len)] + [full(a) for a in weights],
        out_specs=[row] * 8,
        out_shape=[jax.ShapeDtypeStruct((t_rows, D_MODEL), F32)] * 8,
        compiler_params=_params("parallel"),
        name="rwkv_prep",
    )(x, x, x, mod, mod, *weights)


def _wkv_scan_kernel(*refs, reverse, tt):
    if reverse:
        (r_ref, k_ref, v_ref, w_ref, as_ref, kkp_ref, kap_ref, rkp_ref, gng_ref, gnb_ref, yin_ref, bin_ref,
         y_ref, s_ref, na_ref, b_ref, kd_ref) = refs
    else:
        (r_ref, k_ref, v_ref, w_ref, as_ref, kkp_ref, kap_ref, rkp_ref,
         y_ref, bout_ref, s_ref, na_ref, b_ref, kd_ref) = refs
    n = RWKV_HEAD

    @pl.when(pl.program_id(1) == 0)
    def _():
        s_ref[...] = jnp.zeros_like(s_ref)

    r = r_ref[...]
    k = k_ref[...]
    a_sig = as_ref[...]
    kk = k * kkp_ref[...][None]
    kk = kk / jnp.maximum(jnp.sqrt(jnp.sum(kk * kk, 1, keepdims=True)), 1e-12)
    kd = k * (1.0 + (a_sig - 1.0) * kap_ref[...][None])
    na_ref[...] = -kk
    b_ref[...] = kk * a_sig
    kd_ref[...] = kd
    bonus = jnp.sum(r * kd * rkp_ref[...][None], 1, keepdims=True)

    def time_index(j):
        return tt - 1 - j if reverse else j

    a_first = na_ref[time_index(0)]
    sa0 = jnp.zeros((n, LANES), F32)
    for kc in range(n):
        sa0 = sa0 + s_ref[kc] * a_first[kc:kc + 1, :]

    def step(j, sa):
        t = time_index(j)
        a_next = na_ref[time_index(jnp.minimum(j + 1, tt - 1))]
        w_t = w_ref[t]
        b_t = b_ref[t]
        kd_t = kd_ref[t]
        r_t = r_ref[t]
        v_t = v_ref[t]
        y = jnp.zeros((n, LANES), F32)
        sa_next = jnp.zeros((n, LANES), F32)
        for kc in range(n):
            s_new = s_ref[kc] * w_t[kc:kc + 1, :] + sa * b_t[kc:kc + 1, :] + v_t * kd_t[kc:kc + 1, :]
            s_ref[kc] = s_new
            y = y + s_new * r_t[kc:kc + 1, :]
            sa_next = sa_next + s_new * a_next[kc:kc + 1, :]
        y_ref[t] = y
        return sa_next

    lax.fori_loop(0, tt, step, sa0)

    if reverse:
        y = y_ref[...] + yin_ref[...]
        mean = jnp.mean(y, 1, keepdims=True)
        yc = y - mean
        var = jnp.mean(yc * yc, 1, keepdims=True)
        yn = yc * lax.rsqrt(var + RWKV_GN_EPS) * gng_ref[...][None] + gnb_ref[...][None]
        y_ref[...] = yn + (bonus + bin_ref[...]) * v_ref[...]
    else:
        bout_ref[...] = bonus


def wkv_scan(r, k, v, w, a_sig, chan, *, reverse, y_in=None, bonus_in=None, tt=64):
    seq_len, n, chains = r.shape
    nt = seq_len // tt
    tmap = (lambda c, j: (nt - 1 - j, 0, c)) if reverse else (lambda c, j: (j, 0, c))
    blk = pl.BlockSpec((tt, n, LANES), tmap)
    bblk = pl.BlockSpec((tt, 1, LANES), tmap)
    par = pl.BlockSpec((n, LANES), lambda c, j: (0, 0))
    in_specs = [blk] * 5 + [par] * 3
    args = [r, k, v, w, a_sig, chan["k_k"], chan["k_a"], chan["r_k"]]
    if reverse:
        in_specs += [par, par, blk, bblk]
        args += [chan["gn_g"], chan["gn_b"], y_in, bonus_in]
        out_specs = blk
        out_shape = jax.ShapeDtypeStruct(r.shape, F32)
    else:
        out_specs = [blk, bblk]
        out_shape = [jax.ShapeDtypeStruct(r.shape, F32), jax.ShapeDtypeStruct((seq_len, 1, chains), F32)]
    return pl.pallas_call(
        functools.partial(_wkv_scan_kernel, reverse=reverse, tt=tt),
        grid=(chains // LANES, nt),
        in_specs=in_specs,
        out_specs=out_specs,
        out_shape=out_shape,
        scratch_shapes=[pltpu.VMEM((n, n, LANES), F32)] + [pltpu.VMEM((tt, n, LANES), F32)] * 3,
        compiler_params=_params("parallel", "arbitrary"),
        name="wkv_scan_rev" if reverse else "wkv_scan_fwd",
    )(*args)


def rwkv_mixer(x, mod, rp, nb, seq_len):
    r, k, v, dec0, dec1, as0, as1, g = rwkv_prep(x, mod, rp, seq_len)
    chains = nb * RWKV_HEADS

    def to_chains(a):
        return a.reshape(nb, seq_len, RWKV_HEADS, RWKV_HEAD).transpose(1, 3, 0, 2).reshape(seq_len, RWKV_HEAD, chains)

    rt, kt, vt = to_chains(r), to_chains(k), to_chains(v)
    y_f, bonus_f = wkv_scan(rt, kt, vt, to_chains(dec0), to_chains(as0), rp["chan"][0], reverse=False)
    y = wkv_scan(rt, kt, vt, to_chains(dec1), to_chains(as1), rp["chan"][1], reverse=True,
                 y_in=y_f, bonus_in=bonus_f)
    y = y.reshape(seq_len, RWKV_HEAD, nb, RWKV_HEADS).transpose(2, 0, 3, 1).reshape(nb * seq_len, D_MODEL)
    return y, g


def _diff_attn_kernel(lam_ref, g_ref, q_ref, k_ref, v_ref, o_ref, *, lam_init):
    lf = lam_ref[...]
    lam = (jnp.exp(jnp.sum(lf[0:1] * lf[1:2], -1, keepdims=True))
           - jnp.exp(jnp.sum(lf[2:3] * lf[3:4], -1, keepdims=True)) + lam_init)
    q = q_ref[...]
    k = k_ref[...]

    def softmax_terms(cols):
        s = lax.dot_general(q[:, cols], k[:, cols], (((1,), (1,)), ((), ())), preferred_element_type=F32)
        e = jnp.exp2(s - jnp.max(s, -1, keepdims=True))
        return e, 1.0 / jnp.sum(e, -1, keepdims=True)

    e1, inv1 = softmax_terms(slice(0, DA_HEAD))
    e2, inv2 = softmax_terms(slice(DA_HEAD, 2 * DA_HEAD))
    a = e1 * inv1 - e2 * (lam * inv2)
    o = _dot(a.astype(BF16), v_ref[...])
    o = o * lax.rsqrt(jnp.mean(o * o, -1, keepdims=True) + SUBLN_EPS) * g_ref[...] * (1.0 - lam_init)
    o_ref[...] = o.astype(o_ref.dtype)


def diff_attention(qkv, lam, subln_g, layer_idx, nb, seq_len, *, tq=256):
    lam_init = 0.8 - 0.6 * math.exp(-0.3 * layer_idx)
    hd = 2 * DA_HEAD
    nq = seq_len // tq
    return pl.pallas_call(
        functools.partial(_diff_attn_kernel, lam_init=lam_init),
        grid=(nb, DA_HEADS, nq),
        in_specs=[pl.BlockSpec((4, DA_HEAD), lambda b, h, i: (0, 0)),
                  pl.BlockSpec((1, hd), lambda b, h, i: (0, 0)),
                  pl.BlockSpec((tq, hd), lambda b, h, i: (b * nq + i, h)),
                  pl.BlockSpec((seq_len, hd), lambda b, h, i: (b, DA_HEADS + h)),
                  pl.BlockSpec((seq_len, hd), lambda b, h, i: (b, 2 * DA_HEADS + h))],
        out_specs=pl.BlockSpec((tq, hd), lambda b, h, i: (b * nq + i, h)),
        out_shape=jax.ShapeDtypeStruct((nb * seq_len, D_MODEL), BF16),
        compiler_params=_params("parallel", "parallel", "arbitrary"),
        name="diff_attention",
    )(lam, subln_g.reshape(1, hd), qkv, qkv, qkv)


def _chan_table(p):
    t = p.reshape(RWKV_HEADS, RWKV_HEAD).T
    return jnp.tile(t, (1, LANES // RWKV_HEADS))


def _prepare_weights(P):
    bf = lambda a: a.astype(BF16)
    W = {"hy": [], "rw": [], "da": []}
    for j in range(P["hy_w_in"].shape[0]):
        W["hy"].append(dict(
            w_in=bf(P["hy_w_in"][j]), conv_w=P["hy_conv_w"][j], conv_b=P["hy_conv_b"][j],
            f_w1p=jnp.pad(P["hy_f_w1"][j], ((0, LANES - P["hy_f_w1"].shape[1]), (0, 0))),
            f_b1=P["hy_f_b1"][j], f_w2=P["hy_f_w2"][j], f_b2=P["hy_f_b2"][j], f_w3=P["hy_f_w3"][j],
            f_b3=P["hy_f_b3"][j], f_w4=P["hy_f_w4"][j], f_freq=P["hy_f_freq"][j], skip=P["hy_bias"][j],
            w_out=bf(P["hy_w_out"][j])))
    for j in range(P["rw_w_rkv"].shape[0]):
        zero = jnp.zeros((RWKV_LORA, D_MODEL), F32)
        blockdiag = lambda m: jnp.concatenate([jnp.concatenate([m[0], zero], 1), jnp.concatenate([zero, m[1]], 1)], 0)
        gpad = 2 * LANES - GATE_LORA
        W["rw"].append(dict(
            mu=P["rw_mu"][j], w_rkv=bf(P["rw_w_rkv"][j]),
            w1=bf(jnp.concatenate([P["rw_w1"][j, 0], P["rw_w1"][j, 1]], 1)), w2=bf(blockdiag(P["rw_w2"][j])),
            a1=bf(jnp.concatenate([P["rw_a1"][j, 0], P["rw_a1"][j, 1]], 1)), a2=bf(blockdiag(P["rw_a2"][j])),
            g1=bf(jnp.pad(P["rw_g1"][j], ((0, 0), (0, gpad)))), g2=bf(jnp.pad(P["rw_g2"][j], ((0, gpad), (0, 0)))),
            w0=P["rw_w0"][j], a0=P["rw_a0"][j],
            chan=[dict(k_k=_chan_table(P["rw_k_k"][j]), k_a=_chan_table(P["rw_k_a"][j]),
                       r_k=_chan_table(P["rw_r_k"][j, d]), gn_g=_chan_table(P["rw_gn_g"][j]),
                       gn_b=_chan_table(P["rw_gn_b"][j])) for d in range(2)],
            w_out=bf(P["rw_w_out"][j])))
    for j in range(P["da_w_qkv"].shape[0]):
        W["da"].append(dict(w_qkv=bf(P["da_w_qkv"][j]), lam=P["da_lam"][j], subln_g=P["da_subln_g"][j],
                            w_out=bf(P["da_w_out"][j])))
    W["ff_w_gu"], W["ff_w_down"] = bf(P["ff_w_gu"]), bf(P["ff_w_down"])
    W["moe_w_gu"], W["moe_w_down"] = bf(P["moe_w_gu"]), bf(P["moe_w_down"])
    return W


def _trunk(x, mods, P, W):
    nb, seq_len, _ = x.shape
    x = x.reshape(nb * seq_len, D_MODEL)
    for i in range(DEPTH):
        mod = mods[i].reshape(nb * 6, 1, D_MODEL)
        kind, j = i % N_MIXERS, i // N_MIXERS
        zgate = None
        if kind == 0:
            z = hyena_mixer(x, mod, W["hy"][j], nb, seq_len)
            w_out = W["hy"][j]["w_out"]
        elif kind == 1:
            z, zgate = rwkv_mixer(x, mod, W["rw"][j], nb, seq_len)
            w_out = W["rw"][j]["w_out"]
        else:
            da = W["da"][j]
            qkv = qkv_rope_projection(x, mod, da["w_qkv"], seq_len)
            z = diff_attention(qkv, da["lam"], da["subln_g"], i, nb, seq_len)
            w_out = da["w_out"]
        x = out_projection_ln(z, x, mod, w_out, P["ln_g"][i, 0], P["ln_b"][i, 0], seq_len, zgate=zgate)
        if i % 2 == 0:
            x = swiglu_ln(x, mod, W["ff_w_gu"][i // 2], W["ff_w_down"][i // 2],
                          P["ln_g"][i, 1], P["ln_b"][i, 1], seq_len)
        else:
            x = moe_ln(x, mod, P["moe_router"][i // 2], W["moe_w_gu"][i // 2], W["moe_w_down"][i // 2],
                       P["ln_g"][i, 1], P["ln_b"][i, 1], seq_len)
    return x.reshape(nb, seq_len, D_MODEL)


def kernel(x_prompt, x_sample, c_prompt, c_sample, ada_w, ada_b, ln_g, ln_b, hy_w_in, hy_conv_w, hy_conv_b, hy_f_w1, hy_f_b1, hy_f_w2, hy_f_b2, hy_f_w3, hy_f_b3, hy_f_w4, hy_f_freq, hy_bias, hy_w_out, rw_mu, rw_w_rkv, rw_w0, rw_w1, rw_w2, rw_a0, rw_a1, rw_a2, rw_g1, rw_g2, rw_k_k, rw_k_a, rw_r_k, rw_gn_g, rw_gn_b, rw_w_out, da_w_qkv, da_lam, da_subln_g, da_w_out, ff_w_gu, ff_w_down, moe_router, moe_w_gu, moe_w_down):
    P = dict(ada_w=ada_w, ada_b=ada_b, ln_g=ln_g, ln_b=ln_b,
             hy_w_in=hy_w_in, hy_conv_w=hy_conv_w, hy_conv_b=hy_conv_b,
             hy_f_w1=hy_f_w1, hy_f_b1=hy_f_b1, hy_f_w2=hy_f_w2, hy_f_b2=hy_f_b2,
             hy_f_w3=hy_f_w3, hy_f_b3=hy_f_b3, hy_f_w4=hy_f_w4, hy_f_freq=hy_f_freq,
             hy_bias=hy_bias, hy_w_out=hy_w_out,
             rw_mu=rw_mu, rw_w_rkv=rw_w_rkv, rw_w0=rw_w0, rw_w1=rw_w1, rw_w2=rw_w2,
             rw_a0=rw_a0, rw_a1=rw_a1, rw_a2=rw_a2, rw_g1=rw_g1, rw_g2=rw_g2,
             rw_k_k=rw_k_k, rw_k_a=rw_k_a, rw_r_k=rw_r_k, rw_gn_g=rw_gn_g, rw_gn_b=rw_gn_b,
             rw_w_out=rw_w_out,
             da_w_qkv=da_w_qkv, da_lam=da_lam, da_subln_g=da_subln_g, da_w_out=da_w_out,
             ff_w_gu=ff_w_gu, ff_w_down=ff_w_down,
             moe_router=moe_router, moe_w_gu=moe_w_gu, moe_w_down=moe_w_down)
    W = _prepare_weights(P)
    n_prompt = x_prompt.shape[0]
    mods = ada_modulation(jnp.concatenate([c_prompt, c_sample], 0), ada_w, ada_b)
    y_prompt = _trunk(x_prompt, mods[:, :n_prompt], P, W)
    y_sample = _trunk(x_sample, mods[:, n_prompt:], P, W)
    return (y_prompt, y_sample)
```

```python
import functools
import math

import jax
import jax.numpy as jnp
from jax import lax
from jax.experimental import pallas as pl
from jax.experimental.pallas import tpu as pltpu

F32 = jnp.float32
BF16 = jnp.bfloat16

D_MODEL = 1024
DEPTH = 4
N_MIXERS = 3
DEEPNORM_ALPHA = (2 * DEPTH) ** 0.25
LN_EPS = 1e-5

HYENA_ORDER = 2
FILTER_BANDS = 16
FILTER_HIDDEN = 64
HYENA_MIN_DECAY = math.log(1e-2) / 1.5
HYENA_MAX_DECAY = math.log(1e-2) / 0.3
FILTER_NORM_EPS = 1e-6

RWKV_HEAD = 64
RWKV_HEADS = D_MODEL // RWKV_HEAD
RWKV_LORA = 64
GATE_LORA = 160
RWKV_GN_EPS = 64e-5

DA_HEAD = 64
DA_HEADS = D_MODEL // (2 * DA_HEAD)
ROPE_THETA = 10000.0
SUBLN_EPS = 1e-5

D_FF = 2816
N_EXPERTS = 8
D_FF_EXPERT = 3584

LANES = 128
VMEM_LIMIT_BYTES = 56 * 1024 * 1024


def _params(*semantics):
    return pltpu.CompilerParams(dimension_semantics=semantics, vmem_limit_bytes=VMEM_LIMIT_BYTES)


def _split_bf16(x):
    hi = x.astype(BF16)
    lo = (x - hi.astype(F32)).astype(BF16)
    return hi, lo


def _dot(a, b):
    return jnp.dot(a, b, preferred_element_type=F32)


def _dot_split(a, b):
    ah, al = _split_bf16(a)
    bh, bl = _split_bf16(b)
    return _dot(ah, bh) + _dot(al, bh) + _dot(ah, bl)


def _sigmoid(x):
    return 1.0 / (1.0 + jnp.exp(-x))


def _layer_norm(x, g, b):
    mu = jnp.mean(x, -1, keepdims=True)
    xc = x - mu
    var = jnp.mean(xc * xc, -1, keepdims=True)
    return xc * lax.rsqrt(var + LN_EPS) * g + b


def _mod_spec(which, tm, seq_len):
    return pl.BlockSpec((1, 1, D_MODEL), lambda i, *_: ((i * tm // seq_len) * 6 + which, 0, 0))


def _ada_kernel(c_ref, w_ref, b_ref, o_ref):
    c = c_ref[...]
    cs = (c * _sigmoid(c)).astype(BF16)
    o_ref[...] = _dot(cs, w_ref[...].astype(BF16)) + b_ref[...]


def ada_modulation(c, ada_w, ada_b):
    nb = c.shape[0]
    tn = 1024
    return pl.pallas_call(
        _ada_kernel,
        grid=(DEPTH, 6 * D_MODEL // tn),
        in_specs=[pl.BlockSpec((nb, D_MODEL), lambda l, j: (0, 0)),
                  pl.BlockSpec((None, D_MODEL, tn), lambda l, j: (l, 0, j)),
                  pl.BlockSpec((None, 1, tn), lambda l, j: (l, 0, j))],
        out_specs=pl.BlockSpec((None, nb, tn), lambda l, j: (l, 0, j)),
        out_shape=jax.ShapeDtypeStruct((DEPTH, nb, 6 * D_MODEL), F32),
        compiler_params=_params("parallel", "parallel"),
        name="ada_modulation",
    )(c, ada_w, ada_b.reshape(DEPTH, 1, 6 * D_MODEL))


def _proj_kernel(x_ref, sc_ref, sh_ref, w_ref, o_ref, xb_ref):
    @pl.when(pl.program_id(1) == 0)
    def _():
        xb_ref[...] = (x_ref[...] * (1.0 + sc_ref[0]) + sh_ref[0]).astype(BF16)

    o_ref[...] = _dot(xb_ref[...], w_ref[...]).astype(o_ref.dtype)


def _qkv_rope_kernel(x_ref, sc_ref, sh_ref, w_ref, cos_ref, sin_ref, o_ref, xb_ref, *, n_rope_tiles):
    j = pl.program_id(1)

    @pl.when(j == 0)
    def _():
        xb_ref[...] = (x_ref[...] * (1.0 + sc_ref[0]) + sh_ref[0]).astype(BF16)

    y = _dot(xb_ref[...], w_ref[...])
    tn = y.shape[1]

    @pl.when(j < n_rope_tiles)
    def _():
        lane = lax.broadcasted_iota(jnp.int32, y.shape, 1)
        first_half = (lane % DA_HEAD) < (DA_HEAD // 2)
        partner = jnp.where(first_half,
                            pltpu.roll(y, tn - DA_HEAD // 2, axis=1),
                            pltpu.roll(y, DA_HEAD // 2, axis=1))
        o_ref[...] = (y * cos_ref[...] + partner * sin_ref[...]).astype(o_ref.dtype)

    @pl.when(j >= n_rope_tiles)
    def _():
        o_ref[...] = y.astype(o_ref.dtype)


def modulated_projection(x, mod, w, seq_len, *, tm=1024, tn=1024):
    t_rows, n_out = x.shape[0], w.shape[1]
    return pl.pallas_call(
        _proj_kernel,
        grid=(t_rows // tm, n_out // tn),
        in_specs=[pl.BlockSpec((tm, D_MODEL), lambda i, j: (i, 0)),
                  _mod_spec(1, tm, seq_len), _mod_spec(0, tm, seq_len),
                  pl.BlockSpec((D_MODEL, tn), lambda i, j: (0, j))],
        out_specs=pl.BlockSpec((tm, tn), lambda i, j: (i, j)),
        out_shape=jax.ShapeDtypeStruct((t_rows, n_out), F32),
        scratch_shapes=[pltpu.VMEM((tm, D_MODEL), BF16)],
        compiler_params=_params("parallel", "arbitrary"),
        name="modulated_projection",
    )(x, mod, mod, w)


def qkv_rope_projection(x, mod, w, seq_len, *, tm=1024, tn=512):
    t_rows = x.shape[0]
    tiles_per_seq = seq_len // tm
    nq_tiles = D_MODEL // tn
    half = DA_HEAD // 2
    inv = ROPE_THETA ** (-jnp.arange(0, DA_HEAD, 2, dtype=F32) / DA_HEAD)
    ang = jnp.arange(seq_len, dtype=F32)[:, None] * inv[None, :]
    cos_t = jnp.tile(jnp.concatenate([jnp.cos(ang), jnp.cos(ang)], -1), (1, tn // DA_HEAD))
    sin_t = jnp.tile(jnp.concatenate([-jnp.sin(ang), jnp.sin(ang)], -1), (1, tn // DA_HEAD))
    qscale = jnp.float32(DA_HEAD ** -0.5 * math.log2(math.e))
    cos_all = jnp.concatenate([cos_t * qscale, cos_t], 0)
    sin_all = jnp.concatenate([sin_t * qscale, sin_t], 0)
    tab = lambda i, j: (jnp.where(j < nq_tiles, 0, tiles_per_seq) + i % tiles_per_seq, 0)
    return pl.pallas_call(
        functools.partial(_qkv_rope_kernel, n_rope_tiles=2 * nq_tiles),
        grid=(t_rows // tm, 3 * D_MODEL // tn),
        in_specs=[pl.BlockSpec((tm, D_MODEL), lambda i, j: (i, 0)),
                  _mod_spec(1, tm, seq_len), _mod_spec(0, tm, seq_len),
                  pl.BlockSpec((D_MODEL, tn), lambda i, j: (0, j)),
                  pl.BlockSpec((tm, tn), tab), pl.BlockSpec((tm, tn), tab)],
        out_specs=pl.BlockSpec((tm, tn), lambda i, j: (i, j)),
        out_shape=jax.ShapeDtypeStruct((t_rows, 3 * D_MODEL), BF16),
        scratch_shapes=[pltpu.VMEM((tm, D_MODEL), BF16)],
        compiler_params=_params("parallel", "arbitrary"),
        name="qkv_rope_projection",
    )(x, mod, mod, w, cos_all, sin_all)


def _out_ln_kernel(*refs, has_zgate):
    if has_zgate:
        z_ref, zg_ref, x_ref, gt_ref, w_ref, g_ref, b_ref, o_ref = refs
        z = (z_ref[...].astype(F32) * zg_ref[...]).astype(BF16)
    else:
        z_ref, x_ref, gt_ref, w_ref, g_ref, b_ref, o_ref = refs
        z = z_ref[...].astype(BF16)
    y = _dot(z, w_ref[...])
    o_ref[...] = _layer_norm(DEEPNORM_ALPHA * x_ref[...] + (1.0 + gt_ref[0]) * y, g_ref[...], b_ref[...])


def out_projection_ln(z, x, mod, w, ln_g, ln_b, seq_len, *, zgate=None, tm=512):
    t_rows = x.shape[0]
    row = pl.BlockSpec((tm, D_MODEL), lambda i: (i, 0))
    vec = pl.BlockSpec((1, D_MODEL), lambda i: (0, 0))
    in_specs = [row] + ([row] if zgate is not None else []) + [
        row, _mod_spec(2, tm, seq_len), pl.BlockSpec((D_MODEL, D_MODEL), lambda i: (0, 0)), vec, vec]
    args = [z] + ([zgate] if zgate is not None else []) + [
        x, mod, w, ln_g.reshape(1, D_MODEL), ln_b.reshape(1, D_MODEL)]
    return pl.pallas_call(
        functools.partial(_out_ln_kernel, has_zgate=zgate is not None),
        grid=(t_rows // tm,),
        in_specs=in_specs,
        out_specs=row,
        out_shape=jax.ShapeDtypeStruct((t_rows, D_MODEL), F32),
        compiler_params=_params("parallel"),
        name="out_projection_ln",
    )(*args)


def _ffn_kernel(x_ref, sc_ref, sh_ref, gt_ref, wg_ref, wu_ref, wd_ref, g_ref, b_ref, o_ref, hb_ref, acc_ref):
    f = pl.program_id(1)

    @pl.when(f == 0)
    def _():
        hb_ref[...] = (x_ref[...] * (1.0 + sc_ref[0]) + sh_ref[0]).astype(BF16)
        acc_ref[...] = jnp.zeros_like(acc_ref)

    hb = hb_ref[...]
    gate = _dot(hb, wg_ref[...])
    up = _dot(hb, wu_ref[...])
    act = (gate * _sigmoid(gate) * up).astype(BF16)
    acc_ref[...] += _dot(act, wd_ref[...])

    @pl.when(f == pl.num_programs(1) - 1)
    def _():
        o_ref[...] = _layer_norm(DEEPNORM_ALPHA * x_ref[...] + (1.0 + gt_ref[0]) * acc_ref[...],
                                 g_ref[...], b_ref[...])


def swiglu_ln(x, mod, w_gu, w_down, ln_g, ln_b, seq_len, *, tm=512, tf=1408):
    t_rows = x.shape[0]
    nf = D_FF // tf
    row = pl.BlockSpec((tm, D_MODEL), lambda i, f: (i, 0))
    vec = pl.BlockSpec((1, D_MODEL), lambda i, f: (0, 0))
    return pl.pallas_call(
        _ffn_kernel,
        grid=(t_rows // tm, nf),
        in_specs=[row, _mod_spec(4, tm, seq_len), _mod_spec(3, tm, seq_len), _mod_spec(5, tm, seq_len),
                  pl.BlockSpec((D_MODEL, tf), lambda i, f: (0, f)),
                  pl.BlockSpec((D_MODEL, tf), lambda i, f: (0, nf + f)),
                  pl.BlockSpec((tf, D_MODEL), lambda i, f: (f, 0)),
                  vec, vec],
        out_specs=row,
        out_shape=jax.ShapeDtypeStruct((t_rows, D_MODEL), F32),
        scratch_shapes=[pltpu.VMEM((tm, D_MODEL), BF16), pltpu.VMEM((tm, D_MODEL), F32)],
        compiler_params=_params("parallel", "arbitrary"),
        name="swiglu_ln",
    )(x, mod, mod, mod, w_gu, w_gu, w_down, ln_g.reshape(1, D_MODEL), ln_b.reshape(1, D_MODEL))


MOE_SUB = 144


def _moe_kernel(x_ref, sc_ref, sh_ref, gt_ref, r_ref, wg_ref, wu_ref, wd_ref, g_ref, b_ref, o_ref,
                hb_ref, comb_ref, slot_ref, slot_t_ref, cnt_ref, xg_ref, yg_ref, acc_ref):
    e = pl.program_id(1)
    f = pl.program_id(2)
    tm = x_ref.shape[0]
    sub = MOE_SUB

    @pl.when((e == 0) & (f == 0))
    def _():
        h = x_ref[...] * (1.0 + sc_ref[0]) + sh_ref[0]
        hb_ref[...] = h.astype(BF16)
        acc_ref[...] = jnp.zeros_like(acc_ref)
        logits = _dot_split(h, r_ref[...])
        lane = lax.broadcasted_iota(jnp.int32, logits.shape, 1)
        neg = jnp.float32(-jnp.inf)
        lg = jnp.where(lane < N_EXPERTS, logits, neg)
        m1 = jnp.max(lg, -1, keepdims=True)
        i1 = jnp.min(jnp.where(lg == m1, lane, LANES), -1, keepdims=True)
        lg2 = jnp.where(lane == i1, neg, lg)
        m2 = jnp.max(lg2, -1, keepdims=True)
        i2 = jnp.min(jnp.where(lg2 == m2, lane, LANES), -1, keepdims=True)
        e2 = jnp.exp(m2 - m1)
        g1 = 1.0 / (1.0 + e2)
        comb = jnp.where(lane == i1, g1, 0.0) + jnp.where(lane == i2, e2 * g1, 0.0)
        comb_ref[...] = comb
        sel = comb > 0.0
        selb = jnp.where(sel, 1.0, 0.0).astype(BF16)
        chunk = 256
        for r0 in range(0, tm, chunk):
            rr = lax.broadcasted_iota(jnp.int32, (chunk, tm), 0) + r0
            cc = lax.broadcasted_iota(jnp.int32, (chunk, tm), 1)
            before = jnp.where(cc < rr, 1.0, 0.0).astype(BF16)
            slot_ref[r0:r0 + chunk, :] = _dot(before, selb)
        slot = jnp.where(sel, slot_ref[...], -1.0)
        slot_ref[...] = slot
        slot_t_ref[...] = slot.T
        cnt_ref[...] = jnp.sum(jnp.where(sel, 1.0, 0.0), 0, keepdims=True)

    lane1 = lax.broadcasted_iota(jnp.int32, cnt_ref.shape, 1)
    n_e = jnp.sum(jnp.where(lane1 == e, cnt_ref[...], 0.0)).astype(jnp.int32)
    n_sub = (n_e + sub - 1) // sub

    def for_sub_blocks(fn):
        def pair(s, carry):
            fn(pl.multiple_of(s * (2 * sub), 2 * sub), 2 * sub)
            return carry

        lax.fori_loop(0, n_sub // 2, pair, 0)

        @pl.when(n_sub % 2 == 1)
        def _():
            fn(pl.multiple_of((n_sub - 1) * sub, sub), sub)

    @pl.when(f == 0)
    def _():
        slot_row = slot_t_ref[pl.ds(e, 1), :]

        def gather(base, rows):
            want = (lax.broadcasted_iota(jnp.int32, (rows, tm), 0) + base).astype(F32)
            pick = jnp.where(slot_row == want, 1.0, 0.0).astype(BF16)
            xg_ref[pl.ds(base, rows), :] = _dot(pick, hb_ref[...]).astype(BF16)
            yg_ref[pl.ds(base, rows), :] = jnp.zeros((rows, D_MODEL), F32)

        for_sub_blocks(gather)

    def expert_ffn(base, rows):
        xs = xg_ref[pl.ds(base, rows), :]
        gate = _dot(xs, wg_ref[...])
        up = _dot(xs, wu_ref[...])
        act = (gate * _sigmoid(gate) * up).astype(BF16)
        yg_ref[pl.ds(base, rows), :] += _dot(act, wd_ref[...])

    for_sub_blocks(expert_ffn)

    @pl.when(f == pl.num_programs(2) - 1)
    def _():
        lane = lax.broadcasted_iota(jnp.int32, comb_ref.shape, 1)
        mine = lane == e
        slot_col = jnp.sum(jnp.where(mine, slot_ref[...], 0.0), -1, keepdims=True)
        c_e = jnp.sum(jnp.where(mine, comb_ref[...], 0.0), -1, keepdims=True)

        def scatter(base, rows):
            have = (lax.broadcasted_iota(jnp.int32, (tm, rows), 1) + base).astype(F32)
            put = jnp.where(slot_col == have, 1.0, 0.0).astype(BF16)
            acc_ref[...] += c_e * _dot(put, yg_ref[pl.ds(base, rows), :].astype(BF16))

        for_sub_blocks(scatter)

    @pl.when((e == pl.num_programs(1) - 1) & (f == pl.num_programs(2) - 1))
    def _():
        o_ref[...] = _layer_norm(DEEPNORM_ALPHA * x_ref[...] + (1.0 + gt_ref[0]) * acc_ref[...],
                                 g_ref[...], b_ref[...])


def moe_ln(x, mod, router, w_gu, w_down, ln_g, ln_b, seq_len, *, tm=1024, tf=896):
    t_rows = x.shape[0]
    nf = D_FF_EXPERT // tf
    router_p = jnp.pad(router, ((0, 0), (0, LANES - N_EXPERTS)))
    slots = -(-tm // MOE_SUB) * MOE_SUB
    row = pl.BlockSpec((tm, D_MODEL), lambda i, e, f: (i, 0))
    vec = pl.BlockSpec((1, D_MODEL), lambda i, e, f: (0, 0))
    return pl.pallas_call(
        _moe_kernel,
        grid=(t_rows // tm, N_EXPERTS, nf),
        in_specs=[row, _mod_spec(4, tm, seq_len), _mod_spec(3, tm, seq_len), _mod_spec(5, tm, seq_len),
                  pl.BlockSpec((D_MODEL, LANES), lambda i, e, f: (0, 0)),
                  pl.BlockSpec((None, D_MODEL, tf), lambda i, e, f: (e, 0, f)),
                  pl.BlockSpec((None, D_MODEL, tf), lambda i, e, f: (e, 0, nf + f)),
                  pl.BlockSpec((None, tf, D_MODEL), lambda i, e, f: (e, f, 0)),
                  vec, vec],
        out_specs=row,
        out_shape=jax.ShapeDtypeStruct((t_rows, D_MODEL), F32),
        scratch_shapes=[pltpu.VMEM((tm, D_MODEL), BF16),
                        pltpu.VMEM((tm, LANES), F32),
                        pltpu.VMEM((tm, LANES), F32),
                        pltpu.VMEM((LANES, tm), F32),
                        pltpu.VMEM((1, LANES), F32),
                        pltpu.VMEM((slots, D_MODEL), BF16),
                        pltpu.VMEM((slots, D_MODEL), F32),
                        pltpu.VMEM((tm, D_MODEL), F32)],
        compiler_params=_params("parallel", "arbitrary", "arbitrary"),
        name="moe_ln",
    )(x, mod, mod, mod, router_p, w_gu, w_gu, w_down, ln_g.reshape(1, D_MODEL), ln_b.reshape(1, D_MODEL))


def _short_conv_kernel(u_ref, w_ref, b_ref, o_ref):
    u = u_ref[...]
    n = u.shape[0]
    row = lax.broadcasted_iota(jnp.int32, u.shape, 0)
    prev = jnp.where(row == 0, 0.0, pltpu.roll(u, 1, axis=0))
    nxt = jnp.where(row == n - 1, 0.0, pltpu.roll(u, n - 1, axis=0))
    o_ref[...] = prev * w_ref[0:1, :] + u * w_ref[1:2, :] + nxt * w_ref[2:3, :] + b_ref[...]


def short_conv(u, conv_w, conv_b, *, tn=256):
    nb, seq_len, width = u.shape
    blk = pl.BlockSpec((None, seq_len, tn), lambda b, j: (b, 0, j))
    return pl.pallas_call(
        _short_conv_kernel,
        grid=(nb, width // tn),
        in_specs=[blk, pl.BlockSpec((3, tn), lambda b, j: (0, j)), pl.BlockSpec((1, tn), lambda b, j: (0, j))],
        out_specs=blk,
        out_shape=jax.ShapeDtypeStruct(u.shape, F32),
        compiler_params=_params("parallel", "parallel"),
        name="short_conv",
    )(u, conv_w, conv_b.reshape(1, width))


def _filter_kernel(z_ref, w1_ref, b1_ref, w2_ref, b2_ref, w3_ref, b3_ref, w4_ref, fr_ref, dl_ref,
                   sum_ref, dif_ref, nrm_ref, nyq_ref):
    i = pl.program_id(0)
    z = z_ref[...]
    fr = fr_ref[...]
    h = jnp.sin(fr * (_dot_split(z, w1_ref[...]) + b1_ref[...]))
    h = jnp.sin(fr * (_dot_split(h, w2_ref[...]) + b2_ref[...]))
    h = jnp.sin(fr * (_dot_split(h, w3_ref[...]) + b3_ref[...]))
    hf = _dot_split(h, w4_ref[...])
    win = jnp.exp(-z[:, 0:1] * dl_ref[...])
    win = jnp.concatenate([win] * HYENA_ORDER, axis=1)
    half = HYENA_ORDER * D_MODEL
    row = lax.broadcasted_iota(jnp.int32, (z.shape[0], half), 0) + i * z.shape[0]
    kf = hf[:, :half] * win
    kb = jnp.where(row == 0, 0.0, hf[:, half:] * win)
    ksum = kf + kb
    sum_ref[...] = ksum.astype(sum_ref.dtype)
    dif_ref[...] = (kf - kb).astype(dif_ref.dtype)
    sign = (1 - 2 * (row % 2)).astype(F32)

    @pl.when(i == 0)
    def _():
        nrm_ref[...] = jnp.zeros_like(nrm_ref)
        nyq_ref[...] = jnp.zeros_like(nyq_ref)

    nrm_ref[...] += jnp.sum(jnp.abs(kf) + jnp.abs(kb), 0, keepdims=True)
    nyq_ref[...] += jnp.sum(ksum * sign, 0, keepdims=True)


def hyena_filter_taps(feat, w1p, b1, w2, b2, w3, b3, w4, freq, deltas, *, tl=256):
    seq_len = feat.shape[0]
    half = HYENA_ORDER * D_MODEL
    full = lambda a: pl.BlockSpec(a.shape, lambda i: (0,) * a.ndim)
    args = [feat, w1p, b1.reshape(1, -1), w2, b2.reshape(1, -1), w3, b3.reshape(1, -1), w4,
            freq.reshape(1, -1), deltas.reshape(1, -1)]
    return pl.pallas_call(
        _filter_kernel,
        grid=(seq_len // tl,),
        in_specs=[pl.BlockSpec((tl, LANES), lambda i: (i, 0))] + [full(a) for a in args[1:]],
        out_specs=[pl.BlockSpec((tl, half), lambda i: (i, 0)), pl.BlockSpec((tl, half), lambda i: (i, 0)),
                   pl.BlockSpec((1, half), lambda i: (0, 0)), pl.BlockSpec((1, half), lambda i: (0, 0))],
        out_shape=[jax.ShapeDtypeStruct((seq_len, half), BF16), jax.ShapeDtypeStruct((seq_len, half), BF16),
                   jax.ShapeDtypeStruct((1, half), F32), jax.ShapeDtypeStruct((1, half), F32)],
        compiler_params=_params("arbitrary"),
        name="hyena_filter_taps",
    )(*args)


def _spectrum_kernel(m_ref, k_ref, nrm_ref, nyq_ref, o_ref, acc_ref, *, patch_row0):
    kk = pl.program_id(2)

    @pl.when(kk == 0)
    def _():
        acc_ref[...] = jnp.zeros_like(acc_ref)

    acc_ref[...] += _dot(m_ref[...], k_ref[...])

    @pl.when(kk == pl.num_programs(2) - 1)
    def _():
        inv = 1.0 / (nrm_ref[...] + FILTER_NORM_EPS)
        out = acc_ref[...] * inv
        if patch_row0:
            row = lax.broadcasted_iota(jnp.int32, out.shape, 0) + pl.program_id(0) * out.shape[0]
            out = jnp.where(row == 0, nyq_ref[...] * inv, out)
        o_ref[...] = out


def filter_spectrum(mat, taps, nrm, nyq, *, patch_row0, tm=512, tn=512, tk=512):
    seq_len, width = taps.shape
    return pl.pallas_call(
        functools.partial(_spectrum_kernel, patch_row0=patch_row0),
        grid=(seq_len // tm, width // tn, seq_len // tk),
        in_specs=[pl.BlockSpec((tm, tk), lambda i, j, k: (i, k)),
                  pl.BlockSpec((tk, tn), lambda i, j, k: (k, j)),
                  pl.BlockSpec((1, tn), lambda i, j, k: (0, j)),
                  pl.BlockSpec((1, tn), lambda i, j, k: (0, j))],
        out_specs=pl.BlockSpec((tm, tn), lambda i, j, k: (i, j)),
        out_shape=jax.ShapeDtypeStruct((seq_len, width), F32),
        scratch_shapes=[pltpu.VMEM((tm, tn), F32)],
        compiler_params=_params("parallel", "parallel", "arbitrary"),
        name="filter_spectrum",
    )(mat, taps, nrm, nyq)


def _long_conv_kernel(z_ref, gate_ref, hr_ref, hi_ref, skip_ref, fc_ref, fs_ref, gc_ref, gs_ref, o_ref,
                      zb_ref, acc_ref):
    f = pl.program_id(2)

    @pl.when(f == 0)
    def _():
        zb_ref[...] = z_ref[...].astype(BF16)
        acc_ref[...] = jnp.zeros_like(acc_ref)

    zb = zb_ref[...]
    xr = _dot(fc_ref[...], zb)
    xi = _dot(fs_ref[...], zb)
    hr = hr_ref[...]
    hi = hi_ref[...]
    row = lax.broadcasted_iota(jnp.int32, xr.shape, 0)
    packed = (row == 0) & (f == 0)
    pr = jnp.where(packed, xr * hr, xr * hr - xi * hi)
    pi = jnp.where(packed, xi * hi, xr * hi + xi * hr)
    acc_ref[...] += _dot(gc_ref[...], pr.astype(BF16)) + _dot(gs_ref[...], pi.astype(BF16))

    @pl.when(f == pl.num_programs(2) - 1)
    def _():
        conv = acc_ref[...] * (1.0 / z_ref.shape[0])
        o_ref[...] = gate_ref[...] * (conv + z_ref[...] * skip_ref[...])


def long_conv_gate(zsrc, z_col0, gsrc, g_col0, spec_r, spec_i, order, skip, mats, *, tn, tf):
    nb, seq_len, _ = zsrc.shape
    fc, fs, gc, gs = mats
    nct = D_MODEL // tn
    return pl.pallas_call(
        _long_conv_kernel,
        grid=(nb, nct, seq_len // tf),
        in_specs=[pl.BlockSpec((None, seq_len, tn), lambda b, c, f: (b, 0, z_col0 + c)),
                  pl.BlockSpec((None, seq_len, tn), lambda b, c, f: (b, 0, g_col0 + c)),
                  pl.BlockSpec((tf, tn), lambda b, c, f: (f, order * nct + c)),
                  pl.BlockSpec((tf, tn), lambda b, c, f: (f, order * nct + c)),
                  pl.BlockSpec((None, 1, tn), lambda b, c, f: (order, 0, c)),
                  pl.BlockSpec((tf, seq_len), lambda b, c, f: (f, 0)),
                  pl.BlockSpec((tf, seq_len), lambda b, c, f: (f, 0)),
                  pl.BlockSpec((seq_len, tf), lambda b, c, f: (0, f)),
                  pl.BlockSpec((seq_len, tf), lambda b, c, f: (0, f))],
        out_specs=pl.BlockSpec((None, seq_len, tn), lambda b, c, f: (b, 0, c)),
        out_shape=jax.ShapeDtypeStruct((nb, seq_len, D_MODEL), F32),
        scratch_shapes=[pltpu.VMEM((seq_len, tn), BF16), pltpu.VMEM((seq_len, tn), F32)],
        compiler_params=_params("parallel", "parallel", "arbitrary"),
        name="long_conv_gate",
    )(zsrc, gsrc, spec_r, spec_i, skip.reshape(HYENA_ORDER, 1, D_MODEL), fc, fs, gc, gs)


def dft_matrices(seq_len):
    idx = jnp.arange(seq_len, dtype=jnp.int32)
    m = (idx[:, None] * idx[None, :]) % (2 * seq_len)
    ang = m.astype(F32) * (math.pi / seq_len)
    cosm = jnp.cos(ang)
    sinm = -jnp.sin(ang)
    alt = (1 - 2 * (idx % 2)).astype(F32)
    fc = cosm
    fs = sinm.at[0, :].set(alt)
    gc = cosm.at[:, 0].set(0.5)
    gs = sinm.at[:, 0].set(0.5 * alt)
    return tuple(a.astype(BF16) for a in (fc, fs, gc, gs))


def filter_features(seq_len):
    t = jnp.linspace(0.0, 1.0, seq_len, dtype=F32)[:, None]
    omega = 2.0 * math.pi * jnp.arange(seq_len, dtype=F32)[:, None] / seq_len
    bands = jnp.linspace(1e-4, FILTER_BANDS - 1, FILTER_BANDS, dtype=F32)[None, :]
    z = jnp.concatenate([t, jnp.cos(bands * omega), -jnp.sin(bands * omega)], -1)
    return jnp.pad(z, ((0, 0), (0, LANES - z.shape[1])))


def hyena_mixer(x, mod, hp, nb, seq_len):
    tn = 512 if seq_len <= 2048 else 256
    u = modulated_projection(x, mod, hp["w_in"], seq_len)
    u = short_conv(u.reshape(nb, seq_len, 3 * D_MODEL), hp["conv_w"], hp["conv_b"])
    mats = dft_matrices(seq_len)
    deltas = jnp.abs(jnp.linspace(HYENA_MIN_DECAY, HYENA_MAX_DECAY, D_MODEL, dtype=F32))
    ksum, kdif, nrm, nyq = hyena_filter_taps(filter_features(seq_len), hp["f_w1p"], hp["f_b1"], hp["f_w2"],
                                             hp["f_b2"], hp["f_w3"], hp["f_b3"], hp["f_w4"], hp["f_freq"], deltas)
    spec_r = filter_spectrum(mats[0], ksum, nrm, nyq, patch_row0=False)
    spec_i = filter_spectrum(mats[1], kdif, nrm, nyq, patch_row0=True)
    nct = D_MODEL // tn
    tf = 512 if seq_len <= 2048 else 256
    z = long_conv_gate(u, 0, u, nct, spec_r, spec_i, 0, hp["skip"], mats, tn=tn, tf=tf)
    z = long_conv_gate(z, 0, u, 2 * nct, spec_r, spec_i, 1, hp["skip"], mats, tn=tn, tf=tf)
    return z.reshape(nb * seq_len, D_MODEL)


def _rwkv_prep_kernel(x_ref, xp_ref, xn_ref, sc_ref, sh_ref, mu_ref, wrkv_ref, w1_ref, w2_ref, a1_ref, a2_ref,
                      g1_ref, g2_ref, w0_ref, a0_ref,
                      r_ref, k_ref, v_ref, dec0_ref, dec1_ref, as0_ref, as1_ref, g_ref, *, seq_len):
    i = pl.program_id(0)
    tm = x_ref.shape[0]
    sc = 1.0 + sc_ref[0]
    sh = sh_ref[0]
    h = x_ref[...] * sc + sh
    row = lax.broadcasted_iota(jnp.int32, h.shape, 0)
    pos = (row + i * tm) % seq_len
    halo_prev = xp_ref[7:8, :] * sc + sh
    halo_next = xn_ref[0:1, :] * sc + sh
    prev = jnp.where(row == 0, halo_prev, pltpu.roll(h, 1, axis=0))
    prev = jnp.where(pos == 0, 0.0, prev)
    nxt = jnp.where(row == tm - 1, halo_next, pltpu.roll(h, tm - 1, axis=0))
    nxt = jnp.where(pos == seq_len - 1, 0.0, nxt)
    xx = 0.5 * (prev + nxt) - h

    def mix(j):
        return (h + xx * mu_ref[j:j + 1, :]).astype(BF16)

    r_ref[...] = _dot(mix(0), wrkv_ref[0])
    k_ref[...] = _dot(mix(1), wrkv_ref[1])
    v_ref[...] = _dot(mix(2), wrkv_ref[2])
    lw = _dot(jnp.tanh(_dot(mix(3), w1_ref[...])).astype(BF16), w2_ref[...])
    la = _dot(_dot(mix(4), a1_ref[...]).astype(BF16), a2_ref[...])
    g_ref[...] = _dot(_sigmoid(_dot(mix(5), g1_ref[...])).astype(BF16), g2_ref[...])
    for d, (dec_ref, as_ref) in enumerate(((dec0_ref, as0_ref), (dec1_ref, as1_ref))):
        cols = slice(d * D_MODEL, (d + 1) * D_MODEL)
        pre = -(w0_ref[d:d + 1, :] + lw[:, cols])
        softplus = jnp.maximum(pre, 0.0) + jnp.log(1.0 + jnp.exp(-jnp.abs(pre)))
        dec_ref[...] = jnp.exp(-jnp.exp(-softplus - 0.5))
        as_ref[...] = _sigmoid(a0_ref[d:d + 1, :] + la[:, cols])


def rwkv_prep(x, mod, rp, seq_len, *, tm=256):
    t_rows = x.shape[0]
    row = pl.BlockSpec((tm, D_MODEL), lambda i: (i, 0))
    halo = tm // 8
    last8 = t_rows // 8 - 1
    full = lambda a: pl.BlockSpec(a.shape, lambda i: (0,) * a.ndim)
    weights = [rp["mu"], rp["w_rkv"], rp["w1"], rp["w2"], rp["a1"], rp["a2"], rp["g1"], rp["g2"], rp["w0"], rp["a0"]]
    return pl.pallas_call(
        functools.partial(_rwkv_prep_kernel, seq_len=seq_len),
        grid=(t_rows // tm,),
        in_specs=[row,
                  pl.BlockSpec((8, D_MODEL), lambda i: (jnp.maximum(i * halo - 1, 0), 0)),
                  pl.BlockSpec((8, D_MODEL), lambda i: (jnp.minimum((i + 1) * halo, last8), 0)),
                  _mod_spec(1, tm, seq_len), _mod_spec(0, tm, seq_len)] + [full(a) for a in weights],
        out_specs=[row] * 8,
        out_shape=[jax.ShapeDtypeStruct((t_rows, D_MODEL), F32)] * 8,
        compiler_params=_params("parallel"),
        name="rwkv_prep",
    )(x, x, x, mod, mod, *weights)


def _wkv_scan_kernel(*refs, reverse, tt):
    if reverse:
        (r_ref, k_ref, v_ref, w_ref, as_ref, kkp_ref, kap_ref, rkp_ref, gng_ref, gnb_ref, yin_ref, bin_ref,
         y_ref, s_ref, na_ref, b_ref, kd_ref) = refs
    else:
        (r_ref, k_ref, v_ref, w_ref, as_ref, kkp_ref, kap_ref, rkp_ref,
         y_ref, bout_ref, s_ref, na_ref, b_ref, kd_ref) = refs
    n = RWKV_HEAD

    @pl.when(pl.program_id(1) == 0)
    def _():
        s_ref[...] = jnp.zeros_like(s_ref)

    r = r_ref[...]
    k = k_ref[...]
    a_sig = as_ref[...]
    kk = k * kkp_ref[...][None]
    kk = kk / jnp.maximum(jnp.sqrt(jnp.sum(kk * kk, 1, keepdims=True)), 1e-12)
    kd = k * (1.0 + (a_sig - 1.0) * kap_ref[...][None])
    na_ref[...] = -kk
    b_ref[...] = kk * a_sig
    kd_ref[...] = kd
    bonus = jnp.sum(r * kd * rkp_ref[...][None], 1, keepdims=True)

    def time_index(j):
        return tt - 1 - j if reverse else j

    a_first = na_ref[time_index(0)]
    sa0 = jnp.zeros((n, LANES), F32)
    for kc in range(n):
        sa0 = sa0 + s_ref[kc] * a_first[kc:kc + 1, :]

    def step(j, sa):
        t = time_index(j)
        a_next = na_ref[time_index(jnp.minimum(j + 1, tt - 1))]
        w_t = w_ref[t]
        b_t = b_ref[t]
        kd_t = kd_ref[t]
        r_t = r_ref[t]
        v_t = v_ref[t]
        y = jnp.zeros((n, LANES), F32)
        sa_next = jnp.zeros((n, LANES), F32)
        for kc in range(n):
            s_new = s_ref[kc] * w_t[kc:kc + 1, :] + sa * b_t[kc:kc + 1, :] + v_t * kd_t[kc:kc + 1, :]
            s_ref[kc] = s_new
            y = y + s_new * r_t[kc:kc + 1, :]
            sa_next = sa_next + s_new * a_next[kc:kc + 1, :]
        y_ref[t] = y
        return sa_next

    lax.fori_loop(0, tt, step, sa0)

    if reverse:
        y = y_ref[...] + yin_ref[...]
        mean = jnp.mean(y, 1, keepdims=True)
        yc = y - mean
        var = jnp.mean(yc * yc, 1, keepdims=True)
        yn = yc * lax.rsqrt(var + RWKV_GN_EPS) * gng_ref[...][None] + gnb_ref[...][None]
        y_ref[...] = yn + (bonus + bin_ref[...]) * v_ref[...]
    else:
        bout_ref[...] = bonus


def wkv_scan(r, k, v, w, a_sig, chan, *, reverse, y_in=None, bonus_in=None, tt=64):
    seq_len, n, chains = r.shape
    nt = seq_len // tt
    tmap = (lambda c, j: (nt - 1 - j, 0, c)) if reverse else (lambda c, j: (j, 0, c))
    blk = pl.BlockSpec((tt, n, LANES), tmap)
    bblk = pl.BlockSpec((tt, 1, LANES), tmap)
    par = pl.BlockSpec((n, LANES), lambda c, j: (0, 0))
    in_specs = [blk] * 5 + [par] * 3
    args = [r, k, v, w, a_sig, chan["k_k"], chan["k_a"], chan["r_k"]]
    if reverse:
        in_specs += [par, par, blk, bblk]
        args += [chan["gn_g"], chan["gn_b"], y_in, bonus_in]
        out_specs = blk
        out_shape = jax.ShapeDtypeStruct(r.shape, F32)
    else:
        out_specs = [blk, bblk]
        out_shape = [jax.ShapeDtypeStruct(r.shape, F32), jax.ShapeDtypeStruct((seq_len, 1, chains), F32)]
    return pl.pallas_call(
        functools.partial(_wkv_scan_kernel, reverse=reverse, tt=tt),
        grid=(chains // LANES, nt),
        in_specs=in_specs,
        out_specs=out_specs,
        out_shape=out_shape,
        scratch_shapes=[pltpu.VMEM((n, n, LANES), F32)] + [pltpu.VMEM((tt, n, LANES), F32)] * 3,
        compiler_params=_params("parallel", "arbitrary"),
        name="wkv_scan_rev" if reverse else "wkv_scan_fwd",
    )(*args)


def rwkv_mixer(x, mod, rp, nb, seq_len):
    r, k, v, dec0, dec1, as0, as1, g = rwkv_prep(x, mod, rp, seq_len)
    chains = nb * RWKV_HEADS

    def to_chains(a):
        return a.reshape(nb, seq_len, RWKV_HEADS, RWKV_HEAD).transpose(1, 3, 0, 2).reshape(seq_len, RWKV_HEAD, chains)

    rt, kt, vt = to_chains(r), to_chains(k), to_chains(v)
    y_f, bonus_f = wkv_scan(rt, kt, vt, to_chains(dec0), to_chains(as0), rp["chan"][0], reverse=False)
    y = wkv_scan(rt, kt, vt, to_chains(dec1), to_chains(as1), rp["chan"][1], reverse=True,
                 y_in=y_f, bonus_in=bonus_f)
    y = y.reshape(seq_len, RWKV_HEAD, nb, RWKV_HEADS).transpose(2, 0, 3, 1).reshape(nb * seq_len, D_MODEL)
    return y, g


def _diff_attn_kernel(lam_ref, g_ref, q_ref, k_ref, v_ref, o_ref, *, lam_init, tq):
    lf = lam_ref[...]
    lam = (jnp.exp(jnp.sum(lf[0:1] * lf[1:2], -1, keepdims=True))
           - jnp.exp(jnp.sum(lf[2:3] * lf[3:4], -1, keepdims=True)) + lam_init)
    k = k_ref[...]

    def q_tile(i, carry):
        rows = pl.ds(pl.multiple_of(i * tq, tq), tq)
        q = q_ref[rows, :]

        def softmax_terms(cols):
            s = lax.dot_general(q[:, cols], k[:, cols], (((1,), (1,)), ((), ())), preferred_element_type=F32)
            e = jnp.exp2(s - jnp.max(s, -1, keepdims=True))
            return e, 1.0 / jnp.sum(e, -1, keepdims=True)

        e1, inv1 = softmax_terms(slice(0, DA_HEAD))
        e2, inv2 = softmax_terms(slice(DA_HEAD, 2 * DA_HEAD))
        a = e1 * inv1 - e2 * (lam * inv2)
        o = _dot(a.astype(BF16), v_ref[...])
        o = o * lax.rsqrt(jnp.mean(o * o, -1, keepdims=True) + SUBLN_EPS) * g_ref[...] * (1.0 - lam_init)
        o_ref[rows, :] = o.astype(o_ref.dtype)
        return carry

    lax.fori_loop(0, q_ref.shape[0] // tq, q_tile, 0)


def diff_attention(qkv, lam, subln_g, layer_idx, nb, seq_len, *, tq=256):
    lam_init = 0.8 - 0.6 * math.exp(-0.3 * layer_idx)
    hd = 2 * DA_HEAD
    seq = lambda col0: pl.BlockSpec((seq_len, hd), lambda b, h: (b, col0 + h))
    return pl.pallas_call(
        functools.partial(_diff_attn_kernel, lam_init=lam_init, tq=tq),
        grid=(nb, DA_HEADS),
        in_specs=[pl.BlockSpec((4, DA_HEAD), lambda b, h: (0, 0)),
                  pl.BlockSpec((1, hd), lambda b, h: (0, 0)),
                  seq(0), seq(DA_HEADS), seq(2 * DA_HEADS)],
        out_specs=seq(0),
        out_shape=jax.ShapeDtypeStruct((nb * seq_len, D_MODEL), BF16),
        compiler_params=_params("parallel", "parallel"),
        name="diff_attention",
    )(lam, subln_g.reshape(1, hd), qkv, qkv, qkv)


def _chan_table(p):
    t = p.reshape(RWKV_HEADS, RWKV_HEAD).T
    return jnp.tile(t, (1, LANES // RWKV_HEADS))


def _prepare_weights(P):
    bf = lambda a: a.astype(BF16)
    W = {"hy": [], "rw": [], "da": []}
    for j in range(P["hy_w_in"].shape[0]):
        W["hy"].append(dict(
            w_in=bf(P["hy_w_in"][j]), conv_w=P["hy_conv_w"][j], conv_b=P["hy_conv_b"][j],
            f_w1p=jnp.pad(P["hy_f_w1"][j], ((0, LANES - P["hy_f_w1"].shape[1]), (0, 0))),
            f_b1=P["hy_f_b1"][j], f_w2=P["hy_f_w2"][j], f_b2=P["hy_f_b2"][j], f_w3=P["hy_f_w3"][j],
            f_b3=P["hy_f_b3"][j], f_w4=P["hy_f_w4"][j], f_freq=P["hy_f_freq"][j], skip=P["hy_bias"][j],
            w_out=bf(P["hy_w_out"][j])))
    for j in range(P["rw_w_rkv"].shape[0]):
        zero = jnp.zeros((RWKV_LORA, D_MODEL), F32)
        blockdiag = lambda m: jnp.concatenate([jnp.concatenate([m[0], zero], 1), jnp.concatenate([zero, m[1]], 1)], 0)
        gpad = 2 * LANES - GATE_LORA
        W["rw"].append(dict(
            mu=P["rw_mu"][j], w_rkv=bf(P["rw_w_rkv"][j]),
            w1=bf(jnp.concatenate([P["rw_w1"][j, 0], P["rw_w1"][j, 1]], 1)), w2=bf(blockdiag(P["rw_w2"][j])),
            a1=bf(jnp.concatenate([P["rw_a1"][j, 0], P["rw_a1"][j, 1]], 1)), a2=bf(blockdiag(P["rw_a2"][j])),
            g1=bf(jnp.pad(P["rw_g1"][j], ((0, 0), (0, gpad)))), g2=bf(jnp.pad(P["rw_g2"][j], ((0, gpad), (0, 0)))),
            w0=P["rw_w0"][j], a0=P["rw_a0"][j],
            chan=[dict(k_k=_chan_table(P["rw_k_k"][j]), k_a=_chan_table(P["rw_k_a"][j]),
                       r_k=_chan_table(P["rw_r_k"][j, d]), gn_g=_chan_table(P["rw_gn_g"][j]),
                       gn_b=_chan_table(P["rw_gn_b"][j])) for d in range(2)],
            w_out=bf(P["rw_w_out"][j])))
    for j in range(P["da_w_qkv"].shape[0]):
        W["da"].append(dict(w_qkv=bf(P["da_w_qkv"][j]), lam=P["da_lam"][j], subln_g=P["da_subln_g"][j],
                            w_out=bf(P["da_w_out"][j])))
    W["ff_w_gu"], W["ff_w_down"] = bf(P["ff_w_gu"]), bf(P["ff_w_down"])
    W["moe_w_gu"], W["moe_w_down"] = bf(P["moe_w_gu"]), bf(P["moe_w_down"])
    return W


def _trunk(x, mods, P, W):
    nb, seq_len, _ = x.shape
    x = x.reshape(nb * seq_len, D_MODEL)
    for i in range(DEPTH):
        mod = mods[i].reshape(nb * 6, 1, D_MODEL)
        kind, j = i % N_MIXERS, i // N_MIXERS
        zgate = None
        if kind == 0:
            z = hyena_mixer(x, mod, W["hy"][j], nb, seq_len)
            w_out = W["hy"][j]["w_out"]
        elif kind == 1:
            z, zgate = rwkv_mixer(x, mod, W["rw"][j], nb, seq_len)
            w_out = W["rw"][j]["w_out"]
        else:
            da = W["da"][j]
            qkv = qkv_rope_projection(x, mod, da["w_qkv"], seq_len)
            z = diff_attention(qkv, da["lam"], da["subln_g"], i, nb, seq_len)
            w_out = da["w_out"]
        x = out_projection_ln(z, x, mod, w_out, P["ln_g"][i, 0], P["ln_b"][i, 0], seq_len, zgate=zgate)
        if i % 2 == 0:
            x = swiglu_ln(x, mod, W["ff_w_gu"][i // 2], W["ff_w_down"][i // 2],
                          P["ln_g"][i, 1], P["ln_b"][i, 1], seq_len)
        else:
            x = moe_ln(x, mod, P["moe_router"][i // 2], W["moe_w_gu"][i // 2], W["moe_w_down"][i // 2],
                       P["ln_g"][i, 1], P["ln_b"][i, 1], seq_len)
    return x.reshape(nb, seq_len, D_MODEL)


def kernel(x_prompt, x_sample, c_prompt, c_sample, ada_w, ada_b, ln_g, ln_b, hy_w_in, hy_conv_w, hy_conv_b, hy_f_w1, hy_f_b1, hy_f_w2, hy_f_b2, hy_f_w3, hy_f_b3, hy_f_w4, hy_f_freq, hy_bias, hy_w_out, rw_mu, rw_w_rkv, rw_w0, rw_w1, rw_w2, rw_a0, rw_a1, rw_a2, rw_g1, rw_g2, rw_k_k, rw_k_a, rw_r_k, rw_gn_g, rw_gn_b, rw_w_out, da_w_qkv, da_lam, da_subln_g, da_w_out, ff_w_gu, ff_w_down, moe_router, moe_w_gu, moe_w_down):
    P = dict(ada_w=ada_w, ada_b=ada_b, ln_g=ln_g, ln_b=ln_b,
             hy_w_in=hy_w_in, hy_conv_w=hy_conv_w, hy_conv_b=hy_conv_b,
             hy_f_w1=hy_f_w1, hy_f_b1=hy_f_b1, hy_f_w2=hy_f_w2, hy_f_b2=hy_f_b2,
             hy_f_w3=hy_f_w3, hy_f_b3=hy_f_b3, hy_f_w4=hy_f_w4, hy_f_freq=hy_f_freq,
             hy_bias=hy_bias, hy_w_out=hy_w_out,
             rw_mu=rw_mu, rw_w_rkv=rw_w_rkv, rw_w0=rw_w0, rw_w1=rw_w1, rw_w2=rw_w2,
             rw_a0=rw_a0, rw_a1=rw_a1, rw_a2=rw_a2, rw_g1=rw_g1, rw_g2=rw_g2,
             rw_k_k=rw_k_k, rw_k_a=rw_k_a, rw_r_k=rw_r_k, rw_gn_g=rw_gn_g, rw_gn_b=rw_gn_b,
             rw_w_out=rw_w_out,
             da_w_qkv=da_w_qkv, da_lam=da_lam, da_subln_g=da_subln_g, da_w_out=da_w_out,
             ff_w_gu=ff_w_gu, ff_w_down=ff_w_down,
             moe_router=moe_router, moe_w_gu=moe_w_gu, moe_w_down=moe_w_down)
    W = _prepare_weights(P)
    n_prompt = x_prompt.shape[0]
    mods = ada_modulation(jnp.concatenate([c_prompt, c_sample], 0), ada_w, ada_b)
    y_prompt = _trunk(x_prompt, mods[:, :n_prompt], P, W)
    y_sample = _trunk(x_sample, mods[:, n_prompt:], P, W)
    return (y_prompt, y_sample)
```
